```python
import math
import jax
import jax.numpy as jnp
from jax import lax
import numpy as np

D_MODEL = 4096
BATCH = 32
SEQ = 256
DEPTH = 1
DEC_BATCH = 2
DEC_SEQ = 1024
PAST_LEN = 256

GRID_W = 64
MIX_W = D_MODEL
S5_W = MIX_W // 2
S5_GROUP_CH = 16
S5_GROUPS = S5_W // S5_GROUP_CH
S5_STATE = 64
MLSTM_HEADS = 4
MLSTM_DV = (MIX_W - S5_W) // MLSTM_HEADS
MLSTM_DK = MLSTM_DV // 2
MLSTM_CHUNK = 64
QK_W = MLSTM_HEADS * MLSTM_DK
V_W = MLSTM_HEADS * MLSTM_DV
N_GATES = 4 * MLSTM_HEADS
OFF_Q = S5_W
OFF_K = OFF_Q + QK_W
OFF_V = OFF_K + QK_W
OFF_O = OFF_V + V_W
OFF_G = OFF_O + V_W
IN_COLS = OFF_G + N_GATES
N_KEYS = 128
N_EXPERTS = N_KEYS * N_KEYS
PEER_HEADS = 8
PEER_TOPK = 16
PEER_DQ = 256
PEER_BLOCK = 128
EPS = 1e-6

kernel_name = 'hymba_s5_mlstm_peer_prefix_flow_step'


def rms_norm(x, g):
    xf = x.astype(jnp.float32)
    y = xf * lax.rsqrt(jnp.mean(xf * xf, axis=-1, keepdims=True) + EPS)
    return (y * g.astype(jnp.float32)).astype(x.dtype)


def to_col_major(x, rows):
    b, l, ch = x.shape
    return x.reshape(b, rows, GRID_W, ch).transpose(0, 2, 1, 3).reshape(b, l, ch)


def from_col_major(x, rows):
    b, l, ch = x.shape
    return x.reshape(b, GRID_W, rows, ch).transpose(0, 2, 1, 3).reshape(b, l, ch)


def s5_discretise(lam_re, lam_im, log_dt, b_re, b_im):
    lam_re = jnp.minimum(lam_re, -1e-4)
    dt = jnp.exp(log_dt)[:, None]
    mag = jnp.exp(lam_re * dt)
    ang = lam_im * dt
    ab_re = mag * jnp.cos(ang)
    ab_im = mag * jnp.sin(ang)
    den = lam_re * lam_re + lam_im * lam_im
    f_re = ((ab_re - 1.0) * lam_re + ab_im * lam_im) / den
    f_im = (ab_im * lam_re - (ab_re - 1.0) * lam_im) / den
    bb_re = f_re[..., None] * b_re - f_im[..., None] * b_im
    bb_im = f_re[..., None] * b_im + f_im[..., None] * b_re
    return ab_re, ab_im, bb_re, bb_im


def _ssm_combine(e1, e2):
    a1r, a1i, b1r, b1i = e1
    a2r, a2i, b2r, b2i = e2
    ar = a1r * a2r - a1i * a2i
    ai = a1r * a2i + a1i * a2r
    br = a2r * b1r - a2i * b1i + b2r
    bi = a2r * b1i + a2i * b1r + b2i
    return ar, ai, br, bi


def s5_scan(u, lam_re, lam_im, log_dt, b_re, b_im, c_re, c_im, s0_re, s0_im):
    f32 = jnp.float32
    ab_re, ab_im, bb_re, bb_im = s5_discretise(lam_re.astype(f32), lam_im.astype(f32), log_dt.astype(f32),
                                               b_re.astype(f32), b_im.astype(f32))
    x_re = jnp.einsum('blgm,gpm->blgp', u, bb_re)
    x_im = jnp.einsum('blgm,gpm->blgp', u, bb_im)
    x_re = x_re.at[:, 0].add(ab_re * s0_re - ab_im * s0_im)
    x_im = x_im.at[:, 0].add(ab_re * s0_im + ab_im * s0_re)
    a_re = jnp.broadcast_to(ab_re, x_re.shape)
    a_im = jnp.broadcast_to(ab_im, x_im.shape)
    _, _, s_re, s_im = lax.associative_scan(_ssm_combine, (a_re, a_im, x_re, x_im), axis=1)
    y = jnp.einsum('gmp,blgp->blgm', c_re.astype(f32), s_re) - jnp.einsum('gmp,blgp->blgm', c_im.astype(f32), s_im)
    return y, s_re[:, -1], s_im[:, -1]


def s5_mixer(u, lp, s0_re, s0_im):
    f32 = jnp.float32
    bn, l, _ = u.shape
    uf = u.astype(f32)
    ug = uf.reshape(bn, l, S5_GROUPS, S5_GROUP_CH)
    ys, fin_re, fin_im = [], [], []
    for d in range(2):
        ud = ug if d == 0 else ug[:, ::-1]
        y, sr, si = s5_scan(ud, lp['s5_lambda_re'][d], lp['s5_lambda_im'][d], lp['s5_log_dt'][d],
                            lp['s5_b_re'][d], lp['s5_b_im'][d], lp['s5_c_re'][d], lp['s5_c_im'][d],
                            s0_re[:, d].astype(f32), s0_im[:, d].astype(f32))
        ys.append(y if d == 0 else y[:, ::-1])
        fin_re.append(sr)
        fin_im.append(si)
    y = (ys[0] + ys[1]).reshape(bn, l, S5_W) + lp['s5_d'].astype(f32) * uf
    z = jax.nn.gelu(y).astype(u.dtype)
    out = z * jax.nn.sigmoid(z @ lp['s5_w_glu'])
    return out, jnp.stack(fin_re, axis=1).astype(u.dtype), jnp.stack(fin_im, axis=1).astype(u.dtype)


def mlstm_chunkwise(q, k, v, ig, fg, c0, n0, m0):
    bn, l, nh, _ = q.shape
    dv = v.shape[-1]
    t_len = MLSTM_CHUNK
    nc = l // t_len

    def chunk(a):
        return a.reshape((bn, nc, t_len) + a.shape[2:]).swapaxes(0, 1).swapaxes(2, 3)

    tril = jnp.tril(jnp.ones((t_len, t_len), dtype=bool))

    def step(carry, inp):
        c_prev, n_prev, m_prev = carry
        qc, kc, vc, ic, fc = inp
        b = jnp.cumsum(jax.nn.log_sigmoid(fc), axis=-1)
        log_w = jnp.where(tril, b[..., :, None] - b[..., None, :] + ic[..., None, :], -jnp.inf)
        log_a = b + m_prev[..., None]
        m_t = jnp.maximum(log_a, jnp.max(log_w, axis=-1))
        w = jnp.exp(log_w - m_t[..., None])
        a = jnp.exp(log_a - m_t)
        s = jnp.einsum('bhtd,bhsd->bhts', qc, kc) * w
        num = a[..., None] * jnp.einsum('bhtd,bhde->bhte', qc, c_prev) + jnp.einsum('bhts,bhse->bhte', s, vc)
        den = a * jnp.einsum('bhtd,bhd->bht', qc, n_prev) + jnp.sum(s, axis=-1)
        h = num / jnp.maximum(jnp.abs(den), jnp.exp(-m_t))[..., None]
        b_end = b[..., -1]
        log_g = b_end[..., None] - b + ic
        m_end = jnp.maximum(b_end + m_prev, jnp.max(log_g, axis=-1))
        a_end = jnp.exp(b_end + m_prev - m_end)
        g = jnp.exp(log_g - m_end[..., None])
        c_new = a_end[..., None, None] * c_prev + jnp.einsum('bhsd,bhse->bhde', kc * g[..., None], vc)
        n_new = a_end[..., None] * n_prev + jnp.einsum('bhs,bhsd->bhd', g, kc)
        return (c_new, n_new, m_end), h

    (c_end, n_end, m_end), hs = lax.scan(step, (c0, n0, m0),
                                         (chunk(q), chunk(k), chunk(v), chunk(ig), chunk(fg)))
    hs = hs.swapaxes(2, 3).swapaxes(0, 1).reshape(bn, l, nh, dv)
    return hs, c_end, n_end, m_end


def mlstm_mixer(q, k, v, o, gates, norm_g, c0, n0, m0):
    f32 = jnp.float32
    bn, l = q.shape[:2]
    c0, n0, m0 = c0.astype(f32), n0.astype(f32), m0.astype(f32)
    flip = lambda a: a[:, ::-1]
    hf, cf, nf, mf = mlstm_chunkwise(q, k, v, gates[:, :, 0], gates[:, :, 1], c0[:, 0], n0[:, 0], m0[:, 0])
    hb, cb, nb, mb = mlstm_chunkwise(flip(q), flip(k), flip(v), flip(gates[:, :, 2]), flip(gates[:, :, 3]),
                                     c0[:, 1], n0[:, 1], m0[:, 1])
    hsum = hf + flip(hb)
    hn = hsum * lax.rsqrt(jnp.mean(hsum * hsum, axis=-1, keepdims=True) + EPS) * norm_g.astype(f32)
    out = o * hn.reshape(bn, l, V_W)
    return out, jnp.stack([cf, cb], 1), jnp.stack([nf, nb], 1), jnp.stack([mf, mb], 1)


def mixer(h, lp, states, rows):
    f32 = jnp.float32
    bn, l, _ = h.shape
    proj = h @ lp['w_in']
    u_s5 = proj[..., :OFF_Q]
    q = proj[..., OFF_Q:OFF_K].astype(f32).reshape(bn, l, MLSTM_HEADS, MLSTM_DK)
    k = proj[..., OFF_K:OFF_V].astype(f32).reshape(bn, l, MLSTM_HEADS, MLSTM_DK) * (MLSTM_DK ** -0.5)
    v = proj[..., OFF_V:OFF_O].astype(f32).reshape(bn, l, MLSTM_HEADS, MLSTM_DV)
    o = jax.nn.sigmoid(proj[..., OFF_O:OFF_G].astype(f32))
    gates = proj[..., OFF_G:].astype(f32).reshape(bn, l, 4, MLSTM_HEADS) + lp['gate_bias'].astype(f32)
    if rows is not None:
        u_s5 = to_col_major(u_s5, rows)
    y_s5, s_re, s_im = s5_mixer(u_s5, lp, states[0], states[1])
    if rows is not None:
        y_s5 = from_col_major(y_s5, rows)
    y_ml, mc, mn, mm = mlstm_mixer(q, k, v, o, gates, lp['mlstm_norm_g'], states[2], states[3], states[4])
    out = jnp.concatenate([y_s5, y_ml.astype(h.dtype)], axis=-1) @ lp['w_out']
    dt_ = h.dtype
    return out, (s_re, s_im, mc.astype(dt_), mn.astype(dt_), mm.astype(dt_))


def peer(h, w_q, keys, u_tab, v_tab):
    bn, l, dm = h.shape
    xt = h.reshape((bn * l) // PEER_BLOCK, PEER_BLOCK, dm)

    def one(xb):
        tb = xb.shape[0]
        qv = (xb @ w_q).reshape(tb, PEER_HEADS, 2, PEER_DQ // 2)
        sc = jnp.einsum('thcd,hckd->thck', qv, keys).astype(jnp.float32)
        v1, i1 = lax.top_k(sc[:, :, 0], PEER_TOPK)
        v2, i2 = lax.top_k(sc[:, :, 1], PEER_TOPK)
        cand = (v1[..., :, None] + v2[..., None, :]).reshape(tb, PEER_HEADS, PEER_TOPK * PEER_TOPK)
        top_s, ci = lax.top_k(cand, PEER_TOPK)
        e = (jnp.take_along_axis(i1, ci // PEER_TOPK, axis=-1) * N_KEYS
             + jnp.take_along_axis(i2, ci % PEER_TOPK, axis=-1))
        g = jax.nn.softmax(top_s, axis=-1)
        act = jax.nn.gelu(jnp.einsum('thkd,td->thk', u_tab[e], xb).astype(jnp.float32))
        return jnp.einsum('thk,thkd->td', (g * act).astype(xb.dtype), v_tab[e])

    return lax.map(one, xt).reshape(bn, l, dm)


def trunk_block(x, mod, lp, states, rows):
    sh1, sc1, g1, sh2, sc2, g2 = jnp.split(mod, 6, axis=-1)
    h = rms_norm(x, lp['norm1_g']) * (1.0 + sc1) + sh1
    m, fin = mixer(h, lp, states, rows)
    x = x + g1 * m
    h = rms_norm(x, lp['norm2_g']) * (1.0 + sc2) + sh2
    x = x + g2 * peer(h, lp['peer_w_q'], lp['peer_keys'], lp['peer_u'], lp['peer_v'])
    return x, fin


def setup_inputs(seed: int = 0) -> dict:
    key = jax.random.key(seed)
    ks = jax.random.split(key, 40)
    f32 = jnp.float32
    nrm = lambda k, shape, s: jax.random.normal(k, shape, f32) * s
    H, DK, DV, G, P, GC = MLSTM_HEADS, MLSTM_DK, MLSTM_DV, S5_GROUPS, S5_STATE, S5_GROUP_CH
    ib = nrm(ks[20], (DEPTH, 2, H), 0.1)
    fb = jnp.linspace(3.0, 6.0, H, dtype=f32) + nrm(ks[21], (DEPTH, 2, H), 0.1)
    gate_bias = jnp.stack([ib[:, 0], fb[:, 0], ib[:, 1], fb[:, 1]], axis=1)
    return {
        'x_prompt': nrm(ks[0], (BATCH, SEQ, D_MODEL), 1.0),
        'x_sample': nrm(ks[1], (DEC_BATCH, DEC_SEQ, D_MODEL), 1.0),
        'state_s5_re': nrm(ks[2], (DEC_BATCH, DEPTH, 2, G, P), 0.3),
        'state_s5_im': nrm(ks[3], (DEC_BATCH, DEPTH, 2, G, P), 0.3),
        'state_mlstm_c': nrm(ks[4], (DEC_BATCH, DEPTH, 2, H, DK, DV), 0.1),
        'state_mlstm_n': nrm(ks[5], (DEC_BATCH, DEPTH, 2, H, DK), 0.1),
        'state_mlstm_m': nrm(ks[6], (DEC_BATCH, DEPTH, 2, H), 1.0),
        'c': nrm(ks[7], (DEC_BATCH, D_MODEL), 1.0),
        'c_ctx': nrm(ks[8], (D_MODEL,), 1.0),
        'w_mod': nrm(ks[9], (DEPTH, D_MODEL, 6 * D_MODEL), 0.5 * D_MODEL ** -0.5),
        'b_mod': nrm(ks[10], (DEPTH, 6 * D_MODEL), 0.02),
        'norm1_g': 1.0 + nrm(ks[11], (DEPTH, D_MODEL), 0.02),
        'norm2_g': 1.0 + nrm(ks[12], (DEPTH, D_MODEL), 0.02),
        'w_in': nrm(ks[13], (DEPTH, D_MODEL, IN_COLS), D_MODEL ** -0.5),
        'gate_bias': gate_bias,
        's5_lambda_re': -0.5 + nrm(ks[14], (DEPTH, 2, G, P), 0.01),
        's5_lambda_im': math.pi * jnp.arange(P, dtype=f32) + nrm(ks[15], (DEPTH, 2, G, P), 0.01),
        's5_log_dt': jax.random.uniform(ks[16], (DEPTH, 2, G), f32, math.log(1e-3), math.log(1e-1)),
        's5_b_re': nrm(ks[17], (DEPTH, 2, G, P, GC), (2 * GC) ** -0.5),
        's5_b_im': nrm(ks[18], (DEPTH, 2, G, P, GC), (2 * GC) ** -0.5),
        's5_c_re': nrm(ks[19], (DEPTH, 2, G, GC, P), 0.5),
        's5_c_im': nrm(ks[22], (DEPTH, 2, G, GC, P), 0.5),
        's5_d': nrm(ks[23], (DEPTH, S5_W), 0.5),
        's5_w_glu': nrm(ks[24], (DEPTH, S5_W, S5_W), S5_W ** -0.5),
        'mlstm_norm_g': 1.0 + nrm(ks[25], (DEPTH, H, DV), 0.02),
        'w_out': nrm(ks[26], (DEPTH, MIX_W, D_MODEL), MIX_W ** -0.5),
        'peer_w_q': nrm(ks[27], (DEPTH, D_MODEL, PEER_HEADS * PEER_DQ), D_MODEL ** -0.5),
        'peer_keys': nrm(ks[28], (DEPTH, PEER_HEADS, 2, N_KEYS, PEER_DQ // 2), (PEER_DQ // 2) ** -0.5),
        'peer_u': nrm(ks[29], (DEPTH, N_EXPERTS, D_MODEL), D_MODEL ** -0.5),
        'peer_v': nrm(ks[30], (DEPTH, N_EXPERTS, D_MODEL), 1.0),
        'final_g': 1.0 + nrm(ks[31], (D_MODEL,), 0.02),
    }


def reference(x_prompt, x_sample, state_s5_re, state_s5_im, state_mlstm_c, state_mlstm_n, state_mlstm_m,
              c, c_ctx, w_mod, b_mod, norm1_g, norm2_g, w_in, gate_bias, s5_lambda_re, s5_lambda_im,
              s5_log_dt, s5_b_re, s5_b_im, s5_c_re, s5_c_im, s5_d, s5_w_glu, mlstm_norm_g, w_out,
              peer_w_q, peer_keys, peer_u, peer_v, final_g):
    rows = x_sample.shape[1] // GRID_W
    bp = x_prompt.shape[0]
    dt_ = x_prompt.dtype
    zero_states = (jnp.zeros((bp, 2, S5_GROUPS, S5_STATE), dt_),
                   jnp.zeros((bp, 2, S5_GROUPS, S5_STATE), dt_),
                   jnp.zeros((bp, 2, MLSTM_HEADS, MLSTM_DK, MLSTM_DV), dt_),
                   jnp.zeros((bp, 2, MLSTM_HEADS, MLSTM_DK), dt_),
                   jnp.zeros((bp, 2, MLSTM_HEADS), dt_))
    xp, xs = x_prompt, x_sample
    ctx_states = []
    for l in range(DEPTH):
        lp = {
            'norm1_g': norm1_g[l], 'norm2_g': norm2_g[l], 'w_in': w_in[l], 'gate_bias': gate_bias[l],
            's5_lambda_re': s5_lambda_re[l], 's5_lambda_im': s5_lambda_im[l], 's5_log_dt': s5_log_dt[l],
            's5_b_re': s5_b_re[l], 's5_b_im': s5_b_im[l], 's5_c_re': s5_c_re[l], 's5_c_im': s5_c_im[l],
            's5_d': s5_d[l], 's5_w_glu': s5_w_glu[l], 'mlstm_norm_g': mlstm_norm_g[l], 'w_out': w_out[l],
            'peer_w_q': peer_w_q[l], 'peer_keys': peer_keys[l], 'peer_u': peer_u[l], 'peer_v': peer_v[l],
        }
        mod_ctx = (jax.nn.silu(c_ctx) @ w_mod[l] + b_mod[l])[None, None, :]
        mod_lat = (jax.nn.silu(c) @ w_mod[l] + b_mod[l])[:, None, :]
        xp, st = trunk_block(xp, mod_ctx, lp, zero_states, None)
        cache_l = (state_s5_re[:, l], state_s5_im[:, l], state_mlstm_c[:, l], state_mlstm_n[:, l], state_mlstm_m[:, l])
        xs, _ = trunk_block(xs, mod_lat, lp, cache_l, rows)
        ctx_states.append(st)
    new_s5_re = jnp.stack([s[0] for s in ctx_states], axis=1)
    new_s5_im = jnp.stack([s[1] for s in ctx_states], axis=1)
    new_mlstm_c = jnp.stack([s[2] for s in ctx_states], axis=1)
    new_mlstm_n = jnp.stack([s[3] for s in ctx_states], axis=1)
    new_mlstm_m = jnp.stack([s[4] for s in ctx_states], axis=1)
    y_prompt = rms_norm(xp, final_g)
    y_sample = rms_norm(xs, final_g)
    return (y_prompt, y_sample, new_s5_re, new_s5_im, new_mlstm_c, new_mlstm_n, new_mlstm_m)
```

```python
import functools
import math

import jax
import jax.numpy as jnp
from jax import lax
from jax.experimental import pallas as pl
from jax.experimental.pallas import tpu as pltpu

F32 = jnp.float32
BF16 = jnp.bfloat16

V7X_VMEM_BYTES = 64 * 1024 * 1024
VMEM_LIMIT = V7X_VMEM_BYTES - 8 * 1024 * 1024
LANES = 128
SUBLANES = 8

EPS = 1e-6
GRID_W = 64
S5_GROUP_CH = 16
S5_STATE = 64
S5_BLOCK_GROUPS = 16
MLSTM_HEADS = 4
MLSTM_T = 256
PEER_HEADS = 8
PEER_TOPK = 16
N_KEYS = 128
NEG_INF = float("-inf")


def _params(*sem):
    return pltpu.CompilerParams(dimension_semantics=sem, vmem_limit_bytes=VMEM_LIMIT)


def _sigmoid(x):
    return 1.0 / (1.0 + jnp.exp(-x))


def _gelu_tanh(x):
    c = math.sqrt(2.0 / math.pi)
    return 0.5 * x * (1.0 + jnp.tanh(c * (x + 0.044715 * (x * x * x))))


def _mod_row(i, tiles_prompt, tiles_per_sample):
    return jnp.where(i < tiles_prompt, 0, 1 + (i - tiles_prompt) // tiles_per_sample)


def _mod_kernel(c_ref, w_ref, b_ref, o_ref):
    c = c_ref[...]
    s = c * _sigmoid(c)
    o_ref[...] = jnp.dot(s, w_ref[...], preferred_element_type=F32) + b_ref[...]


def modulation(cvec, w_mod, b_mod, tn=512):
    k, n = w_mod.shape
    return pl.pallas_call(
        _mod_kernel,
        out_shape=jax.ShapeDtypeStruct((SUBLANES, n), F32),
        grid=(n // tn,),
        in_specs=[pl.BlockSpec((SUBLANES, k), lambda j: (0, 0)),
                  pl.BlockSpec((k, tn), lambda j: (0, j)),
                  pl.BlockSpec((1, tn), lambda j: (0, j))],
        out_specs=pl.BlockSpec((SUBLANES, tn), lambda j: (0, j)),
        compiler_params=_params("arbitrary"),
    )(cvec, w_mod, b_mod.reshape(1, n))


def _norm_mod(x, g, sc, sh):
    y = x * lax.rsqrt(jnp.mean(x * x, axis=-1, keepdims=True) + EPS) * g
    return y * (1.0 + sc) + sh


def _norm1_kernel(tiles_prompt, tiles_per_sample, xp_ref, xs_ref, g_ref, sc_ref, sh_ref, o_ref):
    i = pl.program_id(0)
    r = _mod_row(i, tiles_prompt, tiles_per_sample)
    sc = sc_ref[pl.ds(r, 1), :]
    sh = sh_ref[pl.ds(r, 1), :]

    @pl.when(i < tiles_prompt)
    def _():
        o_ref[...] = _norm_mod(xp_ref[...], g_ref[...], sc, sh).astype(o_ref.dtype)

    @pl.when(i >= tiles_prompt)
    def _():
        o_ref[...] = _norm_mod(xs_ref[...], g_ref[...], sc, sh).astype(o_ref.dtype)


def norm1(xp, xs, g, sc, sh, rows_per_sample, tm=256):
    (n_p, d), n_s = xp.shape, xs.shape[0]
    tp, ts = n_p // tm, n_s // tm
    kern = functools.partial(_norm1_kernel, tp, rows_per_sample // tm)
    return pl.pallas_call(
        kern,
        out_shape=jax.ShapeDtypeStruct((n_p + n_s, d), BF16),
        grid=(tp + ts,),
        in_specs=[pl.BlockSpec((tm, d), lambda i: (jnp.minimum(i, tp - 1), 0)),
                  pl.BlockSpec((tm, d), lambda i: (jnp.maximum(i - tp, 0), 0)),
                  pl.BlockSpec((1, d), lambda i: (0, 0)),
                  pl.BlockSpec((SUBLANES, d), lambda i: (0, 0)),
                  pl.BlockSpec((SUBLANES, d), lambda i: (0, 0))],
        out_specs=pl.BlockSpec((tm, d), lambda i: (i, 0)),
        compiler_params=_params("arbitrary"),
    )(xp, xs, g, sc, sh)


def _norm2_kernel(tiles_prompt, tiles_per_sample, x_ref, g_ref, sc_ref, sh_ref, o_ref):
    r = _mod_row(pl.program_id(0), tiles_prompt, tiles_per_sample)
    o_ref[...] = _norm_mod(x_ref[...], g_ref[...], sc_ref[pl.ds(r, 1), :],
                           sh_ref[pl.ds(r, 1), :]).astype(o_ref.dtype)


def norm2(x, g, sc, sh, n_p, rows_per_sample, tm=256):
    n, d = x.shape
    kern = functools.partial(_norm2_kernel, n_p // tm, rows_per_sample // tm)
    return pl.pallas_call(
        kern,
        out_shape=jax.ShapeDtypeStruct((n, d), BF16),
        grid=(n // tm,),
        in_specs=[pl.BlockSpec((tm, d), lambda i: (i, 0)),
                  pl.BlockSpec((1, d), lambda i: (0, 0)),
                  pl.BlockSpec((SUBLANES, d), lambda i: (0, 0)),
                  pl.BlockSpec((SUBLANES, d), lambda i: (0, 0))],
        out_specs=pl.BlockSpec((tm, d), lambda i: (i, 0)),
        compiler_params=_params("arbitrary"),
    )(x, g, sc, sh)


def _mm_kernel(a_ref, b_ref, o_ref):
    o_ref[...] = jnp.dot(a_ref[...], b_ref[...], preferred_element_type=F32).astype(o_ref.dtype)


def matmul(a, b, tm, tn, out_dtype=F32):
    (m, k), n = a.shape, b.shape[1]
    return pl.pallas_call(
        _mm_kernel,
        out_shape=jax.ShapeDtypeStruct((m, n), out_dtype),
        grid=(m // tm, n // tn),
        in_specs=[pl.BlockSpec((tm, k), lambda i, j: (i, 0)),
                  pl.BlockSpec((k, tn), lambda i, j: (0, j))],
        out_specs=pl.BlockSpec((tm, tn), lambda i, j: (i, j)),
        compiler_params=_params("parallel", "arbitrary"),
    )(a, b)


def _mm_bias_kernel(a_ref, b_ref, bias_ref, o_ref):
    o_ref[...] = jnp.dot(a_ref[...], b_ref[...], preferred_element_type=F32) + bias_ref[...]


def matmul_bias(a, b, bias, tm):
    (m, k), n = a.shape, b.shape[1]
    return pl.pallas_call(
        _mm_bias_kernel,
        out_shape=jax.ShapeDtypeStruct((m, n), F32),
        grid=(m // tm,),
        in_specs=[pl.BlockSpec((tm, k), lambda i: (i, 0)),
                  pl.BlockSpec((k, n), lambda i: (0, 0)),
                  pl.BlockSpec((1, n), lambda i: (0, 0))],
        out_specs=pl.BlockSpec((tm, n), lambda i: (i, 0)),
        compiler_params=_params("arbitrary"),
    )(a, b, bias)


def _s5_disc_kernel(lre_ref, lim_ref, ldt_ref, bre_ref, bim_ref,
                    are_ref, aim_ref, bbre_ref, bbim_ref):
    lam_re = jnp.minimum(lre_ref[...], -1e-4)
    lam_im = lim_ref[...]
    dt = jnp.exp(ldt_ref[...])
    mag = jnp.exp(lam_re * dt)
    ang = lam_im * dt
    ab_re = mag * jnp.cos(ang)
    ab_im = mag * jnp.sin(ang)
    den = lam_re * lam_re + lam_im * lam_im
    f_re = ((ab_re - 1.0) * lam_re + ab_im * lam_im) / den
    f_im = (ab_im * lam_re - (ab_re - 1.0) * lam_im) / den
    are_ref[...] = ab_re
    aim_ref[...] = ab_im
    b_re = bre_ref[...]
    b_im = bim_ref[...]
    bbre_ref[...] = f_re[:, None, :] * b_re - f_im[:, None, :] * b_im
    bbim_ref[...] = f_re[:, None, :] * b_im + f_im[:, None, :] * b_re


def s5_discretise(lam_re, lam_im, log_dt, b_re_t, b_im_t):
    dg, p = lam_re.shape
    gc = b_re_t.shape[1]
    return pl.pallas_call(
        _s5_disc_kernel,
        out_shape=(jax.ShapeDtypeStruct((dg, p), F32), jax.ShapeDtypeStruct((dg, p), F32),
                   jax.ShapeDtypeStruct((dg, gc, p), F32), jax.ShapeDtypeStruct((dg, gc, p), F32)),
    )(lam_re, lam_im, log_dt, b_re_t, b_im_t)


def _s5_scan_kernel(tc, nc, u_ref, bb_ref, cm_ref, are_ref, aim_ref, s0re_ref, s0im_ref,
                    y_ref, finre_ref, finim_ref,
                    xre_ref, xim_ref, sre_ref, sim_ref, cre_ref, cim_ref):
    d = pl.program_id(1)
    c = pl.program_id(3)
    half = xre_ref.shape[1]

    @pl.when(c == 0)
    def _():
        cre_ref[...] = s0re_ref[0, 0]
        cim_ref[...] = s0im_ref[0, 0]

    u = u_ref[...].reshape(tc * SUBLANES, u_ref.shape[2]).astype(BF16)
    x = jnp.dot(u, bb_ref[0, 0], preferred_element_type=F32)
    xre_ref[...] = x[:, :half]
    xim_ref[...] = x[:, half:]

    a_re = jnp.broadcast_to(are_ref[0, 0], (SUBLANES, half))
    a_im = jnp.broadcast_to(aim_ref[0, 0], (SUBLANES, half))

    def step(t, carry):
        s_re, s_im = carry
        te = jnp.where(d == 0, t, tc - 1 - t)
        row = pl.multiple_of(te * SUBLANES, SUBLANES)
        n_re = a_re * s_re - a_im * s_im + xre_ref[pl.ds(row, SUBLANES), :]
        n_im = a_re * s_im + a_im * s_re + xim_ref[pl.ds(row, SUBLANES), :]
        sre_ref[pl.ds(row, SUBLANES), :] = n_re
        sim_ref[pl.ds(row, SUBLANES), :] = n_im
        return n_re, n_im

    s_re, s_im = lax.fori_loop(0, tc, step, (cre_ref[...], cim_ref[...]), unroll=4)
    cre_ref[...] = s_re
    cim_ref[...] = s_im

    y = (jnp.dot(sre_ref[...].astype(BF16), cm_ref[0, 0, :half, :], preferred_element_type=F32)
         + jnp.dot(sim_ref[...].astype(BF16), cm_ref[0, 0, half:, :], preferred_element_type=F32))
    y_ref[0] = y.reshape(tc, SUBLANES, y.shape[1])

    @pl.when(c == nc - 1)
    def _():
        finre_ref[0, 0] = s_re
        finim_ref[0, 0] = s_im


def s5_scan(u_tm, bb, cm, a_re, a_im, s0_re, s0_im, tc=64):
    l, b, w = u_tm.shape
    nj, kb, two_half = bb.shape[1], bb.shape[2], bb.shape[3]
    half = two_half // 2
    nc = l // tc
    nbg = b // SUBLANES
    kern = functools.partial(_s5_scan_kernel, tc, nc)

    def tmap(d, c):
        return jnp.where(d == 0, c, nc - 1 - c)

    return pl.pallas_call(
        kern,
        out_shape=(jax.ShapeDtypeStruct((2, l, b, w), F32),
                   jax.ShapeDtypeStruct((2, nj, b, half), F32),
                   jax.ShapeDtypeStruct((2, nj, b, half), F32)),
        grid=(nbg, 2, nj, nc),
        in_specs=[pl.BlockSpec((tc, SUBLANES, kb), lambda g, d, j, c: (tmap(d, c), g, j)),
                  pl.BlockSpec((1, 1, kb, two_half), lambda g, d, j, c: (d, j, 0, 0)),
                  pl.BlockSpec((1, 1, two_half, kb), lambda g, d, j, c: (d, j, 0, 0)),
                  pl.BlockSpec((1, 1, 1, half), lambda g, d, j, c: (d, j, 0, 0)),
                  pl.BlockSpec((1, 1, 1, half), lambda g, d, j, c: (d, j, 0, 0)),
                  pl.BlockSpec((1, 1, SUBLANES, half), lambda g, d, j, c: (d, j, g, 0)),
                  pl.BlockSpec((1, 1, SUBLANES, half), lambda g, d, j, c: (d, j, g, 0))],
        out_specs=(pl.BlockSpec((1, tc, SUBLANES, kb), lambda g, d, j, c: (d, tmap(d, c), g, j)),
                   pl.BlockSpec((1, 1, SUBLANES, half), lambda g, d, j, c: (d, j, g, 0)),
                   pl.BlockSpec((1, 1, SUBLANES, half), lambda g, d, j, c: (d, j, g, 0))),
        scratch_shapes=[pltpu.VMEM((tc * SUBLANES, half), F32)] * 4
                       + [pltpu.VMEM((SUBLANES, half), F32)] * 2,
        compiler_params=_params("arbitrary", "arbitrary", "arbitrary", "arbitrary"),
    )(u_tm, bb, cm, a_re, a_im, s0_re, s0_im)


def _s5_glu_kernel(tn, yf_ref, yb_ref, u_ref, d_ref, w_ref, o_ref, z_ref):
    j = pl.program_id(1)

    @pl.when(j == 0)
    def _():
        y = yf_ref[...] + yb_ref[...] + d_ref[...] * u_ref[...]
        z_ref[...] = _gelu_tanh(y)

    zb = z_ref[...].astype(BF16)
    acc = jnp.dot(zb, w_ref[...], preferred_element_type=F32)
    col = pl.multiple_of(j * tn, tn)
    o_ref[...] = (z_ref[:, pl.ds(col, tn)] * _sigmoid(acc)).astype(o_ref.dtype)


def s5_glu(yf, yb, proj, d, w_glu, tm=512, tn=512):
    n, w = yf.shape
    kern = functools.partial(_s5_glu_kernel, tn)
    return pl.pallas_call(
        kern,
        out_shape=jax.ShapeDtypeStruct((n, w), BF16),
        grid=(n // tm, w // tn),
        in_specs=[pl.BlockSpec((tm, w), lambda i, j: (i, 0)),
                  pl.BlockSpec((tm, w), lambda i, j: (i, 0)),
                  pl.BlockSpec((tm, w), lambda i, j: (i, 0)),
                  pl.BlockSpec((1, w), lambda i, j: (0, 0)),
                  pl.BlockSpec((w, tn), lambda i, j: (0, j))],
        out_specs=pl.BlockSpec((tm, tn), lambda i, j: (i, j)),
        scratch_shapes=[pltpu.VMEM((tm, w), F32)],
        compiler_params=_params("parallel", "arbitrary"),
    )(yf, yb, proj, d, w_glu)


def _log_sigmoid(x):
    return jnp.minimum(x, 0.0) - jnp.log(1.0 + jnp.exp(-jnp.abs(x)))


def _mlstm_kernel(has_state, nc, dk, dv, *refs):
    t = MLSTM_T
    nh = MLSTM_HEADS
    if has_state:
        (qf_ref, kf_ref, vf_ref, gcf_ref, grf_ref, qb_ref, kb_ref, vb_ref, gcb_ref, grb_ref,
         c0_ref, n0_ref, m0_ref, hf_ref, hb_ref, c_scr, n_scr, m_scr) = refs
        srcs = ((qf_ref, kf_ref, vf_ref, gcf_ref, grf_ref, hf_ref),
                (qb_ref, kb_ref, vb_ref, gcb_ref, grb_ref, hb_ref))
        step = pl.program_id(1)

        @pl.when(step == 0)
        def _():
            c_scr[...] = c0_ref[0]
            n_scr[...] = n0_ref[0]
            m_scr[...] = m0_ref[0]
    else:
        (q_ref, k_ref, v_ref, gc_ref, gr_ref, hf_ref, hb_ref, cend_ref, nend_ref, mend_ref) = refs
        srcs = ((q_ref, k_ref, v_ref, gc_ref, gr_ref, hf_ref),
                (q_ref, k_ref, v_ref, gc_ref, gr_ref, hb_ref))

    ti = lax.broadcasted_iota(jnp.int32, (t, t), 0)
    si = lax.broadcasted_iota(jnp.int32, (t, t), 1)

    for d in range(2):
        q_ref_, k_ref_, v_ref_, gc_ref_, gr_ref_, h_ref_ = srcs[d]
        mask = (si <= ti) if d == 0 else (si >= ti)
        mask_t = (ti <= si) if d == 0 else (ti >= si)
        for h in range(nh):
            gi = (2 * d) * nh + h
            q = q_ref_[:, h * dk:(h + 1) * dk]
            k = k_ref_[:, h * dk:(h + 1) * dk] * (dk ** -0.5)
            v = v_ref_[:, h * dv:(h + 1) * dv]
            qb, kb, vb = q.astype(BF16), k.astype(BF16), v.astype(BF16)
            i_col = gc_ref_[:, gi:gi + 1]
            i_row = gr_ref_[gi:gi + 1, :]
            ls_col = _log_sigmoid(gc_ref_[:, gi + nh:gi + nh + 1])
            ls_row = _log_sigmoid(gr_ref_[gi + nh:gi + nh + 1, :])
            b_col = jnp.sum(jnp.where(mask, ls_row, 0.0), axis=1, keepdims=True)
            b_row = jnp.sum(jnp.where(mask_t, ls_col, 0.0), axis=0, keepdims=True)
            b_end = jnp.sum(ls_row, axis=1, keepdims=True)
            if has_state:
                m_prev = m_scr[d * nh + h:d * nh + h + 1, 0:1]
            else:
                m_prev = jnp.zeros((1, 1), F32)
            log_w = jnp.where(mask, b_col - b_row + i_row, NEG_INF)
            log_a = b_col + m_prev
            m_t = jnp.maximum(log_a, jnp.max(log_w, axis=1, keepdims=True))
            w = jnp.exp(log_w - m_t)
            s = lax.dot_general(qb, kb, (((1,), (1,)), ((), ())), preferred_element_type=F32) * w
            num = jnp.dot(s.astype(BF16), vb, preferred_element_type=F32)
            den = jnp.sum(s, axis=1, keepdims=True)
            if has_state:
                a = jnp.exp(log_a - m_t)
                c_prev = c_scr[d, h]
                n_prev = n_scr[d * nh + h:d * nh + h + 1, :]
                num = num + a * jnp.dot(qb, c_prev.astype(BF16), preferred_element_type=F32)
                den = den + a * jnp.sum(q * n_prev, axis=1, keepdims=True)
            inv = 1.0 / jnp.maximum(jnp.abs(den), jnp.exp(-m_t))
            h_ref_[:, h * dv:(h + 1) * dv] = num * inv

            log_g_col = b_end - b_col + i_col
            log_g_row = b_end - b_row + i_row
            m_end = jnp.maximum(b_end + m_prev, jnp.max(log_g_row, axis=1, keepdims=True))
            g_col = jnp.exp(log_g_col - m_end)
            kg = k * g_col
            c_new = lax.dot_general(kg.astype(BF16), vb, (((0,), (0,)), ((), ())),
                                    preferred_element_type=F32)
            n_new = jnp.sum(kg, axis=0, keepdims=True)
            if has_state:
                a_end = jnp.exp(b_end + m_prev - m_end)
                c_scr[d, h] = a_end * c_prev + c_new
                n_scr[d * nh + h:d * nh + h + 1, :] = a_end * n_prev + n_new
                m_scr[d * nh + h:d * nh + h + 1, :] = jnp.broadcast_to(m_end, (1, LANES))
            else:
                cend_ref[0, d, h] = c_new
                nend_ref[0, d * nh + h:d * nh + h + 1, :] = n_new
                mend_ref[0, d * nh + h:d * nh + h + 1, :] = jnp.broadcast_to(m_end, (1, LANES))


def mlstm(proj, gates_col, gates_row, row0, nb, l, off_q, off_k, off_v, dk, dv, state=None):
    t = MLSTM_T
    nh = MLSTM_HEADS
    n = proj.shape[0]
    nc = l // t
    rb0 = row0 // t
    qk_w, v_w = nh * dk, nh * dv
    has_state = state is not None
    kern = functools.partial(_mlstm_kernel, has_state, nc, dk, dv)
    h_shape = jax.ShapeDtypeStruct((nb * l, v_w), F32)

    def specs(rowmap):
        return [pl.BlockSpec((t, qk_w), lambda b, c: (rowmap(b, c), off_q // qk_w)),
                pl.BlockSpec((t, qk_w), lambda b, c: (rowmap(b, c), off_k // qk_w)),
                pl.BlockSpec((t, v_w), lambda b, c: (rowmap(b, c), off_v // v_w)),
                pl.BlockSpec((t, LANES), lambda b, c: (rowmap(b, c), 0)),
                pl.BlockSpec((2 * SUBLANES, t), lambda b, c: (0, rowmap(b, c)))]

    fwd = lambda b, c: rb0 + b * nc + c
    bwd = lambda b, c: rb0 + b * nc + (nc - 1 - c)
    if has_state:
        c0, n0, m0 = state
        in_specs = specs(fwd) + specs(bwd) + [
            pl.BlockSpec((1, 2, nh, dk, dv), lambda b, c: (b, 0, 0, 0, 0)),
            pl.BlockSpec((1, 2 * nh, dk), lambda b, c: (b, 0, 0)),
            pl.BlockSpec((1, 2 * nh, LANES), lambda b, c: (b, 0, 0))]
        args = (proj, proj, proj, gates_col, gates_row) * 2 + (c0, n0, m0)
        out_shape = (h_shape, h_shape)
        out_specs = (pl.BlockSpec((t, v_w), lambda b, c: (b * nc + c, 0)),
                     pl.BlockSpec((t, v_w), lambda b, c: (b * nc + (nc - 1 - c), 0)))
        scratch = [pltpu.VMEM((2, nh, dk, dv), F32), pltpu.VMEM((2 * nh, dk), F32),
                   pltpu.VMEM((2 * nh, LANES), F32)]
    else:
        assert nc == 1
        in_specs = specs(fwd)
        args = (proj, proj, proj, gates_col, gates_row)
        out_shape = (h_shape, h_shape,
                     jax.ShapeDtypeStruct((nb, 2, nh, dk, dv), F32),
                     jax.ShapeDtypeStruct((nb, 2 * nh, dk), F32),
                     jax.ShapeDtypeStruct((nb, 2 * nh, LANES), F32))
        out_specs = (pl.BlockSpec((t, v_w), lambda b, c: (b, 0)),
                     pl.BlockSpec((t, v_w), lambda b, c: (b, 0)),
                     pl.BlockSpec((1, 2, nh, dk, dv), lambda b, c: (b, 0, 0, 0, 0)),
                     pl.BlockSpec((1, 2 * nh, dk), lambda b, c: (b, 0, 0)),
                     pl.BlockSpec((1, 2 * nh, LANES), lambda b, c: (b, 0, 0)))
        scratch = []
    return pl.pallas_call(
        kern, out_shape=out_shape, grid=(nb, nc), in_specs=in_specs, out_specs=out_specs,
        scratch_shapes=scratch, compiler_params=_params("arbitrary", "arbitrary"),
    )(*args)


def _mlstm_out_kernel(dv, hf_ref, hb_ref, o_ref, g_ref, y_ref):
    for h in range(MLSTM_HEADS):
        sl = slice(h * dv, (h + 1) * dv)
        hs = hf_ref[:, sl] + hb_ref[:, sl]
        hn = hs * lax.rsqrt(jnp.mean(hs * hs, axis=-1, keepdims=True) + EPS) * g_ref[:, sl]
        y_ref[:, sl] = (_sigmoid(o_ref[:, sl]) * hn).astype(y_ref.dtype)


def mlstm_out(hf, hb, proj, off_o, norm_g, dv, tm=512):
    n, w = hf.shape
    return pl.pallas_call(
        functools.partial(_mlstm_out_kernel, dv),
        out_shape=jax.ShapeDtypeStruct((n, w), BF16),
        grid=(n // tm,),
        in_specs=[pl.BlockSpec((tm, w), lambda i: (i, 0)),
                  pl.BlockSpec((tm, w), lambda i: (i, 0)),
                  pl.BlockSpec((tm, w), lambda i: (i, off_o // w)),
                  pl.BlockSpec((1, w), lambda i: (0, 0))],
        out_specs=pl.BlockSpec((tm, w), lambda i: (i, 0)),
        compiler_params=_params("arbitrary"),
    )(hf, hb, proj, norm_g)


def _wout_kernel(tiles_prompt, tiles_per_sample, a1_ref, a2_ref, w1_ref, w2_ref, xp_ref, xs_ref,
                 g_ref, o_ref):
    i = pl.program_id(0)
    r = _mod_row(i, tiles_prompt, tiles_per_sample)
    acc = (jnp.dot(a1_ref[...], w1_ref[...], preferred_element_type=F32)
           + jnp.dot(a2_ref[...], w2_ref[...], preferred_element_type=F32))
    upd = g_ref[pl.ds(r, 1), :] * acc

    @pl.when(i < tiles_prompt)
    def _():
        o_ref[...] = xp_ref[...] + upd

    @pl.when(i >= tiles_prompt)
    def _():
        o_ref[...] = xs_ref[...] + upd


def out_proj_residual(a1, a2, w1, w2, xp, xs, gate, rows_per_sample, tm=512, tn=1024):
    (n, k), d = a1.shape, w1.shape[1]
    tp = xp.shape[0] // tm
    kern = functools.partial(_wout_kernel, tp, rows_per_sample // tm)
    return pl.pallas_call(
        kern,
        out_shape=jax.ShapeDtypeStruct((n, d), F32),
        grid=(n // tm, d // tn),
        in_specs=[pl.BlockSpec((tm, k), lambda i, j: (i, 0)),
                  pl.BlockSpec((tm, k), lambda i, j: (i, 0)),
                  pl.BlockSpec((k, tn), lambda i, j: (0, j)),
                  pl.BlockSpec((k, tn), lambda i, j: (0, j)),
                  pl.BlockSpec((tm, tn), lambda i, j: (jnp.minimum(i, tp - 1), j)),
                  pl.BlockSpec((tm, tn), lambda i, j: (jnp.maximum(i - tp, 0), j)),
                  pl.BlockSpec((SUBLANES, tn), lambda i, j: (0, j))],
        out_specs=pl.BlockSpec((tm, tn), lambda i, j: (i, j)),
        compiler_params=_params("parallel", "arbitrary"),
    )(a1, a2, w1, w2, xp, xs, gate)


def _top_values(x, k):
    vals = []
    for _ in range(k):
        m = jnp.max(x, axis=0, keepdims=True)
        vals.append(m)
        x = jnp.where(x == m, NEG_INF, x)
    return vals


def _peer_route_kernel(qv_ref, keys_ref, s1_ref, e1_ref, s2_ref, e2_ref, thr_ref):
    dq2 = keys_ref.shape[3]
    thr_rows = []
    for h in range(PEER_HEADS):
        sc, tops = [], []
        for c in range(2):
            col = (h * 2 + c) * dq2
            q = qv_ref[:, col:col + dq2].astype(BF16)
            s = lax.dot_general(keys_ref[h, c], q, (((1,), (1,)), ((), ())),
                                preferred_element_type=F32)
            sc.append(s)
            tops.append(_top_values(s, PEER_TOPK))
        v1, v2 = tops
        cands = [v1[p1] + v2[p2] for p1 in range(PEER_TOPK) for p2 in range(PEER_TOPK)
                 if (p1 + 1) * (p2 + 1) <= PEER_TOPK]
        pad = (-len(cands)) % SUBLANES
        cand = jnp.concatenate(cands + [jnp.full_like(cands[0], NEG_INF)] * pad, axis=0)
        best = _top_values(cand, PEER_TOPK)
        top = v1[0] + v2[0]
        z = jnp.ones_like(top)
        for p in range(1, PEER_TOPK):
            z = z + jnp.exp(best[p] - top)
        thr_rows.append(best[PEER_TOPK - 1])
        s1_ref[h] = sc[0]
        s2_ref[h] = sc[1]
        e1_ref[h] = jnp.exp(sc[0] - v1[0]) / z
        e2_ref[h] = jnp.exp(sc[1] - v2[0])
    thr_ref[...] = jnp.concatenate(thr_rows, axis=0)


def peer_route(qv, keys, tt=256):
    n, w = qv.shape
    nh, _, nk, dq2 = keys.shape
    return pl.pallas_call(
        _peer_route_kernel,
        out_shape=(jax.ShapeDtypeStruct((nh, nk, n), F32),) * 4 + (jax.ShapeDtypeStruct((nh, n), F32),),
        grid=(n // tt,),
        in_specs=[pl.BlockSpec((tt, w), lambda i: (i, 0)),
                  pl.BlockSpec((nh, 2, nk, dq2), lambda i: (0, 0, 0, 0))],
        out_specs=(pl.BlockSpec((nh, nk, tt), lambda i: (0, 0, i)),) * 4
                  + (pl.BlockSpec((nh, tt), lambda i: (0, i)),),
        compiler_params=_params("arbitrary"),
    )(qv, keys)


def _peer_dense_kernel(rows, x_ref, u_ref, v_ref, s1_ref, e1_ref, s2_ref, e2_ref, thr_ref,
                       o_ref, wt_ref):
    j = pl.program_id(1)

    @pl.when(j == 0)
    def _():
        o_ref[...] = jnp.zeros_like(o_ref)

    st = lax.dot_general(u_ref[...], x_ref[...], (((1,), (1,)), ((), ())), preferred_element_type=F32)
    nk = s2_ref.shape[1]
    for r in range(rows):
        g = None
        row = j * rows + r
        for h in range(PEER_HEADS):
            ssum = s1_ref[h, pl.ds(row, 1), :] + s2_ref[h]
            wgt = e1_ref[h, pl.ds(row, 1), :] * e2_ref[h]
            term = jnp.where(ssum >= thr_ref[h:h + 1, :], wgt, 0.0)
            g = term if g is None else g + term
        act = _gelu_tanh(st[r * nk:(r + 1) * nk, :])
        wt_ref[r * nk:(r + 1) * nk, :] = (g * act).astype(BF16)
    o_ref[...] += lax.dot_general(wt_ref[...], v_ref[...], (((0,), (0,)), ((), ())),
                                  preferred_element_type=F32)


def peer_dense(x, u_tab, v_tab, s1, e1, s2, e2, thr, tt=512, rows=4):
    n, d = x.shape
    ne = u_tab.shape[0]
    nh, nk = s1.shape[0], s1.shape[1]
    ec = rows * nk
    once = pl.Buffered(1)
    return pl.pallas_call(
        functools.partial(_peer_dense_kernel, rows),
        out_shape=jax.ShapeDtypeStruct((n, d), F32),
        grid=(n // tt, ne // ec),
        in_specs=[pl.BlockSpec((tt, d), lambda i, j: (i, 0), pipeline_mode=once),
                  pl.BlockSpec((ec, d), lambda i, j: (j, 0)),
                  pl.BlockSpec((ec, d), lambda i, j: (j, 0)),
                  pl.BlockSpec((nh, nk, tt), lambda i, j: (0, 0, i), pipeline_mode=once),
                  pl.BlockSpec((nh, nk, tt), lambda i, j: (0, 0, i), pipeline_mode=once),
                  pl.BlockSpec((nh, nk, tt), lambda i, j: (0, 0, i), pipeline_mode=once),
                  pl.BlockSpec((nh, nk, tt), lambda i, j: (0, 0, i), pipeline_mode=once),
                  pl.BlockSpec((nh, tt), lambda i, j: (0, i))],
        out_specs=pl.BlockSpec((tt, d), lambda i, j: (i, 0)),
        scratch_shapes=[pltpu.VMEM((ec, tt), BF16)],
        compiler_params=_params("parallel", "arbitrary"),
    )(x, u_tab, v_tab, s1, e1, s2, e2, thr)


def _final_kernel(tiles_prompt, tiles_per_sample, x_ref, p_ref, g2_ref, fg_ref, yp_ref, ys_ref):
    i = pl.program_id(0)
    r = _mod_row(i, tiles_prompt, tiles_per_sample)
    x = x_ref[...] + g2_ref[pl.ds(r, 1), :] * p_ref[...]
    y = x * lax.rsqrt(jnp.mean(x * x, axis=-1, keepdims=True) + EPS) * fg_ref[...]

    @pl.when(i < tiles_prompt)
    def _():
        yp_ref[...] = y

    @pl.when(i >= tiles_prompt)
    def _():
        ys_ref[...] = y


def final_norm(x1, peer_out, g2, final_g, n_p, rows_per_sample, tm=256):
    n, d = x1.shape
    tp = n_p // tm
    kern = functools.partial(_final_kernel, tp, rows_per_sample // tm)
    return pl.pallas_call(
        kern,
        out_shape=(jax.ShapeDtypeStruct((n_p, d), F32), jax.ShapeDtypeStruct((n - n_p, d), F32)),
        grid=(n // tm,),
        in_specs=[pl.BlockSpec((tm, d), lambda i: (i, 0)),
                  pl.BlockSpec((tm, d), lambda i: (i, 0)),
                  pl.BlockSpec((SUBLANES, d), lambda i: (0, 0)),
                  pl.BlockSpec((1, d), lambda i: (0, 0))],
        out_specs=(pl.BlockSpec((tm, d), lambda i: (jnp.minimum(i, tp - 1), 0)),
                   pl.BlockSpec((tm, d), lambda i: (jnp.maximum(i - tp, 0), 0))),
        compiler_params=_params("arbitrary"),
    )(x1, peer_out, g2, final_g)


def _block_diag_in(bb_re, bb_im):
    nd, g, gc, p = bb_re.shape
    nb = S5_BLOCK_GROUPS
    eye = jnp.eye(nb, dtype=F32)

    def one(b):
        b = b.reshape(nd, g // nb, nb, gc, p)
        return jnp.einsum('djgmp,gh->djgmhp', b, eye).reshape(nd, g // nb, nb * gc, nb * p)

    return jnp.concatenate([one(bb_re), one(bb_im)], axis=-1)


def _block_diag_out(c_re, c_im):
    nd, g, gc, p = c_re.shape
    nb = S5_BLOCK_GROUPS
    eye = jnp.eye(nb, dtype=F32)

    def one(c):
        c = c.reshape(nd, g // nb, nb, gc, p)
        return jnp.einsum('djgmp,gh->djgphm', c, eye).reshape(nd, g // nb, nb * p, nb * gc)

    return jnp.concatenate([one(c_re), one(-c_im)], axis=-2)


def _state_to_blocks(s, bpad):
    b, nd, g, p = s.shape
    nb = S5_BLOCK_GROUPS
    s = s.reshape(b, nd, g // nb, nb * p).transpose(1, 2, 0, 3)
    return jnp.pad(s, ((0, 0), (0, 0), (0, bpad - b), (0, 0)))


def _blocks_to_state(f, b):
    nd, nj, _, hp = f.shape
    return f[:, :, :b].transpose(2, 0, 1, 3).reshape(b, 1, nd, nj * S5_BLOCK_GROUPS, hp // S5_BLOCK_GROUPS)


def kernel(x_prompt, x_sample, state_s5_re, state_s5_im, state_mlstm_c, state_mlstm_n, state_mlstm_m,
           c, c_ctx, w_mod, b_mod, norm1_g, norm2_g, w_in, gate_bias, s5_lambda_re, s5_lambda_im,
           s5_log_dt, s5_b_re, s5_b_im, s5_c_re, s5_c_im, s5_d, s5_w_glu, mlstm_norm_g, w_out,
           peer_w_q, peer_keys, peer_u, peer_v, final_g):
    bp, lp, dm = x_prompt.shape
    bs, ls, _ = x_sample.shape
    n_p, n_s = bp * lp, bs * ls
    rows = ls // GRID_W
    g_s5 = s5_lambda_re.shape[2]
    p_s5 = s5_lambda_re.shape[3]
    s5_w = g_s5 * S5_GROUP_CH
    nh = MLSTM_HEADS
    dk = state_mlstm_c.shape[4]
    dv = state_mlstm_c.shape[5]
    qk_w, v_w = nh * dk, nh * dv
    off_q = s5_w
    off_k = off_q + qk_w
    off_v = off_k + qk_w
    off_o = off_v + v_w
    off_g = off_o + v_w
    n_gates = 4 * nh
    assert bs + 1 <= SUBLANES and lp == MLSTM_T and ls % MLSTM_T == 0

    xp = x_prompt.reshape(n_p, dm)
    xs = x_sample.reshape(n_s, dm)

    cvec = jnp.zeros((SUBLANES, dm), F32).at[0].set(c_ctx).at[1:1 + bs].set(c)
    mod = modulation(cvec, w_mod[0], b_mod[0])
    sh1, sc1, g1, sh2, sc2, g2 = [mod[:, i * dm:(i + 1) * dm] for i in range(6)]

    h1 = norm1(xp, xs, norm1_g, sc1, sh1, ls)
    w_in_b = w_in[0].astype(BF16)
    proj = matmul(h1, w_in_b[:, :off_g], tm=1024, tn=512)
    w_gate = jnp.pad(w_in_b[:, off_g:], ((0, 0), (0, LANES - n_gates)))
    gate_b = jnp.pad(gate_bias[0].reshape(1, n_gates), ((0, 0), (0, LANES - n_gates)))
    gates_col = matmul_bias(h1, w_gate, gate_b, tm=1024)
    gates_row = gates_col[:, :n_gates].T

    lam_re = s5_lambda_re[0].reshape(2 * g_s5, p_s5)
    lam_im = s5_lambda_im[0].reshape(2 * g_s5, p_s5)
    log_dt = s5_log_dt[0].reshape(2 * g_s5, 1)
    b_re_t = s5_b_re[0].transpose(0, 1, 3, 2).reshape(2 * g_s5, S5_GROUP_CH, p_s5)
    b_im_t = s5_b_im[0].transpose(0, 1, 3, 2).reshape(2 * g_s5, S5_GROUP_CH, p_s5)
    ab_re, ab_im, bb_re, bb_im = s5_discretise(lam_re, lam_im, log_dt, b_re_t, b_im_t)
    nj = g_s5 // S5_BLOCK_GROUPS
    half = S5_BLOCK_GROUPS * p_s5
    bb = _block_diag_in(bb_re.reshape(2, g_s5, S5_GROUP_CH, p_s5),
                        bb_im.reshape(2, g_s5, S5_GROUP_CH, p_s5)).astype(BF16)
    cm = _block_diag_out(s5_c_re[0], s5_c_im[0]).astype(BF16)
    a_re = ab_re.reshape(2, nj, 1, half)
    a_im = ab_im.reshape(2, nj, 1, half)

    u_p = proj[:n_p, :s5_w].reshape(bp, lp, s5_w).transpose(1, 0, 2)
    zero_state = jnp.zeros((2, nj, bp, half), F32)
    y_p, fin_re, fin_im = s5_scan(u_p, bb, cm, a_re, a_im, zero_state, zero_state)
    yf_p = y_p[0].transpose(1, 0, 2).reshape(n_p, s5_w)
    yb_p = y_p[1].transpose(1, 0, 2).reshape(n_p, s5_w)

    u_s = proj[n_p:, :s5_w].reshape(bs, rows, GRID_W, s5_w).transpose(2, 1, 0, 3).reshape(ls, bs, s5_w)
    u_s = jnp.pad(u_s, ((0, 0), (0, SUBLANES - bs), (0, 0)))
    s0_re = _state_to_blocks(state_s5_re[:, 0], SUBLANES)
    s0_im = _state_to_blocks(state_s5_im[:, 0], SUBLANES)
    y_s, _, _ = s5_scan(u_s, bb, cm, a_re, a_im, s0_re, s0_im)

    def from_cm(y):
        return y[:, :bs].reshape(GRID_W, rows, bs, s5_w).transpose(2, 1, 0, 3).reshape(n_s, s5_w)

    yf = jnp.concatenate([yf_p, from_cm(y_s[0])], axis=0)
    yb = jnp.concatenate([yb_p, from_cm(y_s[1])], axis=0)
    y_s5 = s5_glu(yf, yb, proj, s5_d, s5_w_glu[0].astype(BF16))

    hf_p, hb_p, c_end, n_end, m_end = mlstm(proj, gates_col, gates_row, 0, bp, lp,
                                            off_q, off_k, off_v, dk, dv)
    m0 = jnp.broadcast_to(state_mlstm_m[:, 0].reshape(bs, 2 * nh, 1), (bs, 2 * nh, LANES))
    hf_s, hb_s = mlstm(proj, gates_col, gates_row, n_p, bs, ls, off_q, off_k, off_v, dk, dv,
                       state=(state_mlstm_c[:, 0], state_mlstm_n[:, 0].reshape(bs, 2 * nh, dk), m0))
    hf = jnp.concatenate([hf_p, hf_s], axis=0)
    hb = jnp.concatenate([hb_p, hb_s], axis=0)
    y_ml = mlstm_out(hf, hb, proj, off_o, mlstm_norm_g[0].reshape(1, v_w), dv)

    w_out_b = w_out[0].astype(BF16)
    x1 = out_proj_residual(y_s5, y_ml, w_out_b[:s5_w], w_out_b[s5_w:], xp, xs, g1, ls)

    h2 = norm2(x1, norm2_g, sc2, sh2, n_p, ls)
    qv = matmul(h2, peer_w_q[0].astype(BF16), tm=1024, tn=512)
    s1, e1, s2, e2, thr = peer_route(qv, peer_keys[0].astype(BF16))
    peer_out = peer_dense(h2, peer_u[0].astype(BF16), peer_v[0].astype(BF16), s1, e1, s2, e2, thr)

    y_p_out, y_s_out = final_norm(x1, peer_out, g2, final_g.reshape(1, dm), n_p, ls)

    dt_ = x_prompt.dtype
    return (y_p_out.reshape(bp, lp, dm), y_s_out.reshape(bs, ls, dm),
            _blocks_to_state(fin_re, bp).astype(dt_), _blocks_to_state(fin_im, bp).astype(dt_),
            c_end.reshape(bp, 1, 2, nh, dk, dv).astype(dt_),
            n_end.reshape(bp, 1, 2, nh, dk).astype(dt_),
            m_end[:, :, 0].reshape(bp, 1, 2, nh).astype(dt_))
```

```python
import functools
import math

import jax
import jax.numpy as jnp
from jax import lax
from jax.experimental import pallas as pl
from jax.experimental.pallas import tpu as pltpu

F32 = jnp.float32
BF16 = jnp.bfloat16

V7X_VMEM_BYTES = 64 * 1024 * 1024
VMEM_LIMIT = V7X_VMEM_BYTES - 8 * 1024 * 1024
LANES = 128
SUBLANES = 8

EPS = 1e-6
GRID_W = 64
S5_GROUP_CH = 16
S5_STATE = 64
S5_BLOCK_GROUPS = 16
S5_SEQ = 256
MLSTM_HEADS = 4
MLSTM_T = 256
PEER_HEADS = 8
PEER_TOPK = 16
N_KEYS = 128
NEG_INF = float("-inf")


def _params(*sem):
    return pltpu.CompilerParams(dimension_semantics=sem, vmem_limit_bytes=VMEM_LIMIT)


def _sigmoid(x):
    return 1.0 / (1.0 + jnp.exp(-x))


def _gelu_tanh(x):
    c = math.sqrt(2.0 / math.pi)
    return 0.5 * x * (1.0 + jnp.tanh(c * (x + 0.044715 * (x * x * x))))


def _mod_row(i, tiles_prompt, tiles_per_sample):
    return jnp.where(i < tiles_prompt, 0, 1 + (i - tiles_prompt) // tiles_per_sample)


def _mod_kernel(c_ref, w_ref, b_ref, o_ref):
    c = c_ref[...]
    s = c * _sigmoid(c)
    o_ref[...] = jnp.dot(s, w_ref[...], preferred_element_type=F32) + b_ref[...]


def modulation(cvec, w_mod, b_mod, tn=512):
    k, n = w_mod.shape
    return pl.pallas_call(
        _mod_kernel,
        out_shape=jax.ShapeDtypeStruct((SUBLANES, n), F32),
        grid=(n // tn,),
        in_specs=[pl.BlockSpec((SUBLANES, k), lambda j: (0, 0)),
                  pl.BlockSpec((k, tn), lambda j: (0, j)),
                  pl.BlockSpec((1, tn), lambda j: (0, j))],
        out_specs=pl.BlockSpec((SUBLANES, tn), lambda j: (0, j)),
        compiler_params=_params("arbitrary"),
    )(cvec, w_mod, b_mod.reshape(1, n))


def _norm_mod(x, g, sc, sh):
    y = x * lax.rsqrt(jnp.mean(x * x, axis=-1, keepdims=True) + EPS) * g
    return y * (1.0 + sc) + sh


def _norm1_kernel(tiles_prompt, tiles_per_sample, xp_ref, xs_ref, g_ref, sc_ref, sh_ref, o_ref):
    i = pl.program_id(0)
    r = _mod_row(i, tiles_prompt, tiles_per_sample)
    sc = sc_ref[pl.ds(r, 1), :]
    sh = sh_ref[pl.ds(r, 1), :]

    @pl.when(i < tiles_prompt)
    def _():
        o_ref[...] = _norm_mod(xp_ref[...], g_ref[...], sc, sh).astype(o_ref.dtype)

    @pl.when(i >= tiles_prompt)
    def _():
        o_ref[...] = _norm_mod(xs_ref[...], g_ref[...], sc, sh).astype(o_ref.dtype)


def norm1(xp, xs, g, sc, sh, rows_per_sample, tm=256):
    (n_p, d), n_s = xp.shape, xs.shape[0]
    tp, ts = n_p // tm, n_s // tm
    kern = functools.partial(_norm1_kernel, tp, rows_per_sample // tm)
    return pl.pallas_call(
        kern,
        out_shape=jax.ShapeDtypeStruct((n_p + n_s, d), BF16),
        grid=(tp + ts,),
        in_specs=[pl.BlockSpec((tm, d), lambda i: (jnp.minimum(i, tp - 1), 0)),
                  pl.BlockSpec((tm, d), lambda i: (jnp.maximum(i - tp, 0), 0)),
                  pl.BlockSpec((1, d), lambda i: (0, 0)),
                  pl.BlockSpec((SUBLANES, d), lambda i: (0, 0)),
                  pl.BlockSpec((SUBLANES, d), lambda i: (0, 0))],
        out_specs=pl.BlockSpec((tm, d), lambda i: (i, 0)),
        compiler_params=_params("arbitrary"),
    )(xp, xs, g, sc, sh)


def _norm2_kernel(tiles_prompt, tiles_per_sample, x_ref, g_ref, sc_ref, sh_ref, o_ref):
    r = _mod_row(pl.program_id(0), tiles_prompt, tiles_per_sample)
    o_ref[...] = _norm_mod(x_ref[...], g_ref[...], sc_ref[pl.ds(r, 1), :],
                           sh_ref[pl.ds(r, 1), :]).astype(o_ref.dtype)


def norm2(x, g, sc, sh, n_p, rows_per_sample, tm=256):
    n, d = x.shape
    kern = functools.partial(_norm2_kernel, n_p // tm, rows_per_sample // tm)
    return pl.pallas_call(
        kern,
        out_shape=jax.ShapeDtypeStruct((n, d), BF16),
        grid=(n // tm,),
        in_specs=[pl.BlockSpec((tm, d), lambda i: (i, 0)),
                  pl.BlockSpec((1, d), lambda i: (0, 0)),
                  pl.BlockSpec((SUBLANES, d), lambda i: (0, 0)),
                  pl.BlockSpec((SUBLANES, d), lambda i: (0, 0))],
        out_specs=pl.BlockSpec((tm, d), lambda i: (i, 0)),
        compiler_params=_params("arbitrary"),
    )(x, g, sc, sh)


def _mm_kernel(a_ref, b_ref, o_ref):
    o_ref[...] = jnp.dot(a_ref[...], b_ref[...].astype(BF16), preferred_element_type=F32)


def matmul(a, b, n, tm, tn):
    m, k = a.shape
    return pl.pallas_call(
        _mm_kernel,
        out_shape=jax.ShapeDtypeStruct((m, n), F32),
        grid=(m // tm, n // tn),
        in_specs=[pl.BlockSpec((tm, k), lambda i, j: (i, 0)),
                  pl.BlockSpec((k, tn), lambda i, j: (0, j))],
        out_specs=pl.BlockSpec((tm, tn), lambda i, j: (i, j)),
        compiler_params=_params("parallel", "arbitrary"),
    )(a, b)


def _mm_bias_kernel(a_ref, b_ref, bias_ref, o_ref):
    o_ref[...] = jnp.dot(a_ref[...], b_ref[...], preferred_element_type=F32) + bias_ref[...]


def matmul_bias(a, b, bias, tm):
    (m, k), n = a.shape, b.shape[1]
    return pl.pallas_call(
        _mm_bias_kernel,
        out_shape=jax.ShapeDtypeStruct((m, n), F32),
        grid=(m // tm,),
        in_specs=[pl.BlockSpec((tm, k), lambda i: (i, 0)),
                  pl.BlockSpec((k, n), lambda i: (0, 0)),
                  pl.BlockSpec((1, n), lambda i: (0, 0))],
        out_specs=pl.BlockSpec((tm, n), lambda i: (i, 0)),
        compiler_params=_params("arbitrary"),
    )(a, b, bias)


def _s5_disc_kernel(lre_ref, lim_ref, ldt_ref, bre_ref, bim_ref,
                    are_ref, aim_ref, bbre_ref, bbim_ref):
    lam_re = jnp.minimum(lre_ref[...], -1e-4)
    lam_im = lim_ref[...]
    dt = jnp.exp(ldt_ref[...])
    mag = jnp.exp(lam_re * dt)
    ang = lam_im * dt
    ab_re = mag * jnp.cos(ang)
    ab_im = mag * jnp.sin(ang)
    den = lam_re * lam_re + lam_im * lam_im
    f_re = ((ab_re - 1.0) * lam_re + ab_im * lam_im) / den
    f_im = (ab_im * lam_re - (ab_re - 1.0) * lam_im) / den
    are_ref[...] = ab_re
    aim_ref[...] = ab_im
    b_re = bre_ref[...]
    b_im = bim_ref[...]
    bbre_ref[...] = f_re[:, None, :] * b_re - f_im[:, None, :] * b_im
    bbim_ref[...] = f_re[:, None, :] * b_im + f_im[:, None, :] * b_re


def s5_discretise(lam_re, lam_im, log_dt, b_re_t, b_im_t):
    dg, p = lam_re.shape
    gc = b_re_t.shape[1]
    return pl.pallas_call(
        _s5_disc_kernel,
        out_shape=(jax.ShapeDtypeStruct((dg, p), F32), jax.ShapeDtypeStruct((dg, p), F32),
                   jax.ShapeDtypeStruct((dg, gc, p), F32), jax.ShapeDtypeStruct((dg, gc, p), F32)),
    )(lam_re, lam_im, log_dt, b_re_t, b_im_t)


def _s5_scan_kernel(with_y, u0_ref, u1_ref, bb_ref, cm_ref, are_ref, aim_ref, s0re_ref, s0im_ref, *rest):
    if with_y:
        y_ref, finre_ref, finim_ref, utb_ref, xre_ref, xim_ref, sre_ref, sim_ref = rest
    else:
        finre_ref, finim_ref, utb_ref, xre_ref, xim_ref = rest
    d = pl.program_id(1)
    l = u0_ref.shape[0] // SUBLANES
    half = xre_ref.shape[1]
    unroll = 8

    def gather(i, carry):
        for k in range(unroll):
            t = i * unroll + k
            row = pl.multiple_of(t * SUBLANES, SUBLANES)
            utb_ref[pl.ds(row, SUBLANES), 0:LANES] = u0_ref[pl.ds(t, SUBLANES, stride=l), :]
            utb_ref[pl.ds(row, SUBLANES), LANES:2 * LANES] = u1_ref[pl.ds(t, SUBLANES, stride=l), :]
        return carry

    lax.fori_loop(0, l // unroll, gather, 0)

    rc = 512
    for r in range(0, l * SUBLANES, rc):
        ub = utb_ref[r:r + rc, :].astype(BF16)
        xre_ref[r:r + rc, :] = jnp.dot(ub, bb_ref[0, 0, :, :half], preferred_element_type=F32)
        xim_ref[r:r + rc, :] = jnp.dot(ub, bb_ref[0, 0, :, half:], preferred_element_type=F32)

    a_re = jnp.broadcast_to(are_ref[0, 0], (SUBLANES, half))
    a_im = jnp.broadcast_to(aim_ref[0, 0], (SUBLANES, half))

    def step(i, carry):
        s_re, s_im = carry
        for k in range(unroll):
            t = i * unroll + k
            te = jnp.where(d == 0, t, l - 1 - t)
            row = pl.multiple_of(te * SUBLANES, SUBLANES)
            n_re = a_re * s_re - a_im * s_im + xre_ref[pl.ds(row, SUBLANES), :]
            n_im = a_re * s_im + a_im * s_re + xim_ref[pl.ds(row, SUBLANES), :]
            if with_y:
                sre_ref[pl.ds(row, SUBLANES), :] = n_re
                sim_ref[pl.ds(row, SUBLANES), :] = n_im
            s_re, s_im = n_re, n_im
        return s_re, s_im

    s_re, s_im = lax.fori_loop(0, l // unroll, step, (s0re_ref[0, 0], s0im_ref[0, 0]))
    finre_ref[0, 0] = s_re
    finim_ref[0, 0] = s_im

    if with_y:
        tr = rc // SUBLANES
        for r in range(0, l * SUBLANES, rc):
            y = (jnp.dot(sre_ref[r:r + rc, :].astype(BF16), cm_ref[0, 0, :half, :], preferred_element_type=F32)
                 + jnp.dot(sim_ref[r:r + rc, :].astype(BF16), cm_ref[0, 0, half:, :],
                           preferred_element_type=F32))
            y_ref[0, r // SUBLANES:r // SUBLANES + tr] = y.reshape(tr, SUBLANES, y.shape[1])


def s5_scan(u, nseq, bb, cm, a_re, a_im, s0_re, s0_im, with_y):
    l = S5_SEQ
    nj, kb, two_half = bb.shape[1], bb.shape[2], bb.shape[3]
    half = two_half // 2
    nbg = nseq // SUBLANES
    rows = l * SUBLANES
    fin = jax.ShapeDtypeStruct((2, nj, nseq, half), F32)
    fin_spec = pl.BlockSpec((1, 1, SUBLANES, half), lambda g, d, j: (d, j, g, 0))
    out_shape, out_specs = (fin, fin), (fin_spec, fin_spec)
    scratch = [pltpu.VMEM((rows, kb), F32)] + [pltpu.VMEM((rows, half), F32)] * 2
    if with_y:
        out_shape = (jax.ShapeDtypeStruct((2, l, nseq, nj * kb), F32),) + out_shape
        out_specs = (pl.BlockSpec((1, l, SUBLANES, kb), lambda g, d, j: (d, 0, g, j)),) + out_specs
        scratch = scratch + [pltpu.VMEM((rows, half), F32)] * 2
    return pl.pallas_call(
        functools.partial(_s5_scan_kernel, with_y),
        out_shape=out_shape,
        grid=(nbg, 2, nj),
        in_specs=[pl.BlockSpec((rows, LANES), lambda g, d, j: (g, 2 * j)),
                  pl.BlockSpec((rows, LANES), lambda g, d, j: (g, 2 * j + 1)),
                  pl.BlockSpec((1, 1, kb, two_half), lambda g, d, j: (d, j, 0, 0)),
                  pl.BlockSpec((1, 1, two_half, kb), lambda g, d, j: (d, j, 0, 0)),
                  pl.BlockSpec((1, 1, 1, half), lambda g, d, j: (d, j, 0, 0)),
                  pl.BlockSpec((1, 1, 1, half), lambda g, d, j: (d, j, 0, 0)),
                  fin_spec, fin_spec],
        out_specs=out_specs,
        scratch_shapes=scratch,
        compiler_params=_params("arbitrary", "arbitrary", "arbitrary"),
    )(u, u, bb, cm, a_re, a_im, s0_re, s0_im)


def _s5_chain_kernel(nb, nseg, fre_ref, fim_ref, are_ref, aim_ref, s0re_ref, s0im_ref, ire_ref, iim_ref):
    nd, nj = fre_ref.shape[0], fre_ref.shape[1]
    for d in range(nd):
        for j in range(nj):
            p_re, p_im = are_ref[d, j], aim_ref[d, j]
            for _ in range(S5_SEQ.bit_length() - 1):
                p_re, p_im = p_re * p_re - p_im * p_im, 2.0 * (p_re * p_im)
            for b in range(nb):
                c_re, c_im = s0re_ref[d, j, b:b + 1, :], s0im_ref[d, j, b:b + 1, :]
                for k in (range(nseg) if d == 0 else reversed(range(nseg))):
                    q = b * nseg + k
                    ire_ref[d, j, q:q + 1, :] = c_re
                    iim_ref[d, j, q:q + 1, :] = c_im
                    f_re, f_im = fre_ref[d, j, q:q + 1, :], fim_ref[d, j, q:q + 1, :]
                    c_re, c_im = (p_re * c_re - p_im * c_im + f_re, p_re * c_im + p_im * c_re + f_im)


def s5_chain(f_re, f_im, a_re, a_im, s0_re, s0_im, nb, nseg):
    assert S5_SEQ & (S5_SEQ - 1) == 0
    shape = jax.ShapeDtypeStruct(f_re.shape, F32)
    return pl.pallas_call(
        functools.partial(_s5_chain_kernel, nb, nseg), out_shape=(shape, shape),
    )(f_re, f_im, a_re, a_im, s0_re, s0_im)


def _s5_glu_kernel(nsub, tn, *refs):
    y_refs, (u_ref, d_ref, w_ref, o_ref, z_ref) = refs[:2 * nsub], refs[2 * nsub:]
    j = pl.program_id(1)

    @pl.when(j == 0)
    def _():
        rs = u_ref.shape[0] // nsub
        for s in range(nsub):
            sl = slice(s * rs, (s + 1) * rs)
            y = y_refs[s][0] + y_refs[nsub + s][0] + d_ref[...] * u_ref[sl, :]
            z_ref[sl, :] = _gelu_tanh(y)

    zb = z_ref[...].astype(BF16)
    acc = jnp.dot(zb, w_ref[...], preferred_element_type=F32)
    col = pl.multiple_of(j * tn, tn)
    o_ref[...] = (z_ref[:, pl.ds(col, tn)] * _sigmoid(acc)).astype(o_ref.dtype)


def s5_glu(y3, nsub, ymap, proj, row_block0, n_rows, d, w_glu, tm=512, tn=512):
    w = w_glu.shape[0]
    rs = tm // nsub
    kern = functools.partial(_s5_glu_kernel, nsub, tn)
    y_specs = [pl.BlockSpec((1, rs, w), functools.partial(lambda i, j, dd, s: (dd,) + ymap(i, s), dd=dd, s=s))
               for dd in range(2) for s in range(nsub)]
    return pl.pallas_call(
        kern,
        out_shape=jax.ShapeDtypeStruct((n_rows, w), BF16),
        grid=(n_rows // tm, w // tn),
        in_specs=y_specs + [pl.BlockSpec((tm, w), lambda i, j: (row_block0 + i, 0)),
                            pl.BlockSpec((1, w), lambda i, j: (0, 0)),
                            pl.BlockSpec((w, tn), lambda i, j: (0, j))],
        out_specs=pl.BlockSpec((tm, tn), lambda i, j: (i, j)),
        scratch_shapes=[pltpu.VMEM((tm, w), F32)],
        compiler_params=_params("parallel", "arbitrary"),
    )(*([y3] * (2 * nsub)), proj, d, w_glu)


def _log_sigmoid(x):
    return jnp.minimum(x, 0.0) - jnp.log(1.0 + jnp.exp(-jnp.abs(x)))


def _mlstm_kernel(has_state, nc, dk, dv, *refs):
    t = MLSTM_T
    nh = MLSTM_HEADS
    if has_state:
        (qf_ref, kf_ref, vf_ref, gcf_ref, grf_ref, qb_ref, kb_ref, vb_ref, gcb_ref, grb_ref,
         c0_ref, n0_ref, m0_ref, hf_ref, hb_ref, c_scr, n_scr, m_scr) = refs
        srcs = ((qf_ref, kf_ref, vf_ref, gcf_ref, grf_ref, hf_ref),
                (qb_ref, kb_ref, vb_ref, gcb_ref, grb_ref, hb_ref))
        step = pl.program_id(1)

        @pl.when(step == 0)
        def _():
            c_scr[...] = c0_ref[0]
            n_scr[...] = n0_ref[0]
            m_scr[...] = m0_ref[0]
    else:
        (q_ref, k_ref, v_ref, gc_ref, gr_ref, hf_ref, hb_ref, cend_ref, nend_ref, mend_ref) = refs
        srcs = ((q_ref, k_ref, v_ref, gc_ref, gr_ref, hf_ref),
                (q_ref, k_ref, v_ref, gc_ref, gr_ref, hb_ref))

    ti = lax.broadcasted_iota(jnp.int32, (t, t), 0)
    si = lax.broadcasted_iota(jnp.int32, (t, t), 1)

    for d in range(2):
        q_ref_, k_ref_, v_ref_, gc_ref_, gr_ref_, h_ref_ = srcs[d]
        mask = (si <= ti) if d == 0 else (si >= ti)
        mask_t = (ti <= si) if d == 0 else (ti >= si)
        for h in range(nh):
            gi = (2 * d) * nh + h
            q = q_ref_[:, h * dk:(h + 1) * dk]
            k = k_ref_[:, h * dk:(h + 1) * dk] * (dk ** -0.5)
            v = v_ref_[:, h * dv:(h + 1) * dv]
            qb, kb, vb = q.astype(BF16), k.astype(BF16), v.astype(BF16)
            i_col = gc_ref_[:, gi:gi + 1]
            i_row = gr_ref_[gi:gi + 1, :]
            ls_col = _log_sigmoid(gc_ref_[:, gi + nh:gi + nh + 1])
            ls_row = _log_sigmoid(gr_ref_[gi + nh:gi + nh + 1, :])
            b_col = jnp.sum(jnp.where(mask, ls_row, 0.0), axis=1, keepdims=True)
            b_row = jnp.sum(jnp.where(mask_t, ls_col, 0.0), axis=0, keepdims=True)
            b_end = jnp.sum(ls_row, axis=1, keepdims=True)
            if has_state:
                m_prev = m_scr[d * nh + h:d * nh + h + 1, 0:1]
            else:
                m_prev = jnp.zeros((1, 1), F32)
            log_w = jnp.where(mask, b_col - b_row + i_row, NEG_INF)
            log_a = b_col + m_prev
            m_t = jnp.maximum(log_a, jnp.max(log_w, axis=1, keepdims=True))
            w = jnp.exp(log_w - m_t)
            s = lax.dot_general(qb, kb, (((1,), (1,)), ((), ())), preferred_element_type=F32) * w
            num = jnp.dot(s.astype(BF16), vb, preferred_element_type=F32)
            den = jnp.sum(s, axis=1, keepdims=True)
            if has_state:
                a = jnp.exp(log_a - m_t)
                c_prev = c_scr[d, h]
                n_prev = n_scr[d * nh + h:d * nh + h + 1, :]
                num = num + a * jnp.dot(qb, c_prev.astype(BF16), preferred_element_type=F32)
                den = den + a * jnp.sum(q * n_prev, axis=1, keepdims=True)
            inv = 1.0 / jnp.maximum(jnp.abs(den), jnp.exp(-m_t))
            h_ref_[:, h * dv:(h + 1) * dv] = num * inv

            log_g_col = b_end - b_col + i_col
            log_g_row = b_end - b_row + i_row
            m_end = jnp.maximum(b_end + m_prev, jnp.max(log_g_row, axis=1, keepdims=True))
            g_col = jnp.exp(log_g_col - m_end)
            kg = k * g_col
            c_new = lax.dot_general(kg.astype(BF16), vb, (((0,), (0,)), ((), ())),
                                    preferred_element_type=F32)
            n_new = jnp.sum(kg, axis=0, keepdims=True)
            if has_state:
                a_end = jnp.exp(b_end + m_prev - m_end)
                c_scr[d, h] = a_end * c_prev + c_new
                n_scr[d * nh + h:d * nh + h + 1, :] = a_end * n_prev + n_new
                m_scr[d * nh + h:d * nh + h + 1, :] = jnp.broadcast_to(m_end, (1, LANES))
            else:
                cend_ref[0, d, h] = c_new
                nend_ref[0, d * nh + h:d * nh + h + 1, :] = n_new
                mend_ref[0, d * nh + h:d * nh + h + 1, :] = jnp.broadcast_to(m_end, (1, LANES))


def mlstm(proj, gates_col, gates_row, row0, nb, l, off_q, off_k, off_v, dk, dv, state=None):
    t = MLSTM_T
    nh = MLSTM_HEADS
    nc = l // t
    rb0 = row0 // t
    qk_w, v_w = nh * dk, nh * dv
    has_state = state is not None
    kern = functools.partial(_mlstm_kernel, has_state, nc, dk, dv)
    h_shape = jax.ShapeDtypeStruct((nb * l, v_w), F32)

    def specs(rowmap):
        return [pl.BlockSpec((t, qk_w), lambda b, c: (rowmap(b, c), off_q // qk_w)),
                pl.BlockSpec((t, qk_w), lambda b, c: (rowmap(b, c), off_k // qk_w)),
                pl.BlockSpec((t, v_w), lambda b, c: (rowmap(b, c), off_v // v_w)),
                pl.BlockSpec((t, LANES), lambda b, c: (rowmap(b, c), 0)),
                pl.BlockSpec((2 * SUBLANES, t), lambda b, c: (0, rowmap(b, c)))]

    fwd = lambda b, c: rb0 + b * nc + c
    bwd = lambda b, c: rb0 + b * nc + (nc - 1 - c)
    if has_state:
        c0, n0, m0 = state
        in_specs = specs(fwd) + specs(bwd) + [
            pl.BlockSpec((1, 2, nh, dk, dv), lambda b, c: (b, 0, 0, 0, 0)),
            pl.BlockSpec((1, 2 * nh, dk), lambda b, c: (b, 0, 0)),
            pl.BlockSpec((1, 2 * nh, LANES), lambda b, c: (b, 0, 0))]
        args = (proj, proj, proj, gates_col, gates_row) * 2 + (c0, n0, m0)
        out_shape = (h_shape, h_shape)
        out_specs = (pl.BlockSpec((t, v_w), lambda b, c: (b * nc + c, 0)),
                     pl.BlockSpec((t, v_w), lambda b, c: (b * nc + (nc - 1 - c), 0)))
        scratch = [pltpu.VMEM((2, nh, dk, dv), F32), pltpu.VMEM((2 * nh, dk), F32),
                   pltpu.VMEM((2 * nh, LANES), F32)]
    else:
        assert nc == 1
        in_specs = specs(fwd)
        args = (proj, proj, proj, gates_col, gates_row)
        out_shape = (h_shape, h_shape,
                     jax.ShapeDtypeStruct((nb, 2, nh, dk, dv), F32),
                     jax.ShapeDtypeStruct((nb, 2 * nh, dk), F32),
                     jax.ShapeDtypeStruct((nb, 2 * nh, LANES), F32))
        out_specs = (pl.BlockSpec((t, v_w), lambda b, c: (b, 0)),
                     pl.BlockSpec((t, v_w), lambda b, c: (b, 0)),
                     pl.BlockSpec((1, 2, nh, dk, dv), lambda b, c: (b, 0, 0, 0, 0)),
                     pl.BlockSpec((1, 2 * nh, dk), lambda b, c: (b, 0, 0)),
                     pl.BlockSpec((1, 2 * nh, LANES), lambda b, c: (b, 0, 0)))
        scratch = []
    return pl.pallas_call(
        kern, out_shape=out_shape, grid=(nb, nc), in_specs=in_specs, out_specs=out_specs,
        scratch_shapes=scratch, compiler_params=_params("arbitrary", "arbitrary"),
    )(*args)


def _mlstm_out_kernel(dv, hf_ref, hb_ref, o_ref, g_ref, y_ref):
    for h in range(MLSTM_HEADS):
        sl = slice(h * dv, (h + 1) * dv)
        hs = hf_ref[:, sl] + hb_ref[:, sl]
        hn = hs * lax.rsqrt(jnp.mean(hs * hs, axis=-1, keepdims=True) + EPS) * g_ref[:, sl]
        y_ref[:, sl] = (_sigmoid(o_ref[:, sl]) * hn).astype(y_ref.dtype)


def mlstm_out(hf, hb, proj, row0, off_o, norm_g, dv, tm=512):
    n, w = hf.shape
    return pl.pallas_call(
        functools.partial(_mlstm_out_kernel, dv),
        out_shape=jax.ShapeDtypeStruct((n, w), BF16),
        grid=(n // tm,),
        in_specs=[pl.BlockSpec((tm, w), lambda i: (i, 0)),
                  pl.BlockSpec((tm, w), lambda i: (i, 0)),
                  pl.BlockSpec((tm, w), lambda i: (row0 // tm + i, off_o // w)),
                  pl.BlockSpec((1, w), lambda i: (0, 0))],
        out_specs=pl.BlockSpec((tm, w), lambda i: (i, 0)),
        compiler_params=_params("arbitrary"),
    )(hf, hb, proj, norm_g)


def _wout_kernel(tiles_prompt, tiles_per_sample, a1p_ref, a1s_ref, a2p_ref, a2s_ref, w1_ref, w2_ref,
                 xp_ref, xs_ref, g_ref, o_ref):
    i = pl.program_id(0)
    gate = g_ref[pl.ds(_mod_row(i, tiles_prompt, tiles_per_sample), 1), :]

    def run(a1_ref, a2_ref, x_ref):
        acc = (jnp.dot(a1_ref[...], w1_ref[...], preferred_element_type=F32)
               + jnp.dot(a2_ref[...], w2_ref[...], preferred_element_type=F32))
        o_ref[...] = x_ref[...] + gate * acc

    @pl.when(i < tiles_prompt)
    def _():
        run(a1p_ref, a2p_ref, xp_ref)

    @pl.when(i >= tiles_prompt)
    def _():
        run(a1s_ref, a2s_ref, xs_ref)


def out_proj_residual(a1p, a1s, a2p, a2s, w1, w2, xp, xs, gate, rows_per_sample, tm=512, tn=1024):
    k, d = w1.shape
    n_p, n_s = xp.shape[0], xs.shape[0]
    tp = n_p // tm
    kern = functools.partial(_wout_kernel, tp, rows_per_sample // tm)
    pmap = lambda i, j: (jnp.minimum(i, tp - 1), 0)
    smap = lambda i, j: (jnp.maximum(i - tp, 0), 0)
    return pl.pallas_call(
        kern,
        out_shape=jax.ShapeDtypeStruct((n_p + n_s, d), F32),
        grid=((n_p + n_s) // tm, d // tn),
        in_specs=[pl.BlockSpec((tm, k), pmap), pl.BlockSpec((tm, k), smap),
                  pl.BlockSpec((tm, k), pmap), pl.BlockSpec((tm, k), smap),
                  pl.BlockSpec((k, tn), lambda i, j: (0, j)),
                  pl.BlockSpec((k, tn), lambda i, j: (0, j)),
                  pl.BlockSpec((tm, tn), lambda i, j: (jnp.minimum(i, tp - 1), j)),
                  pl.BlockSpec((tm, tn), lambda i, j: (jnp.maximum(i - tp, 0), j)),
                  pl.BlockSpec((SUBLANES, tn), lambda i, j: (0, j))],
        out_specs=pl.BlockSpec((tm, tn), lambda i, j: (i, j)),
        compiler_params=_params("parallel", "arbitrary"),
    )(a1p, a1s, a2p, a2s, w1, w2, xp, xs, gate)


def _top_values(x, k):
    vals = []
    for _ in range(k):
        m = jnp.max(x, axis=0, keepdims=True)
        vals.append(m)
        x = jnp.where(x == m, NEG_INF, x)
    return vals


def _peer_route_kernel(qv_ref, keys_ref, s1_ref, e1_ref, s2_ref, e2_ref, thr_ref):
    dq2 = keys_ref.shape[3]
    thr_rows = []
    for h in range(PEER_HEADS):
        sc, tops = [], []
        for c in range(2):
            col = (h * 2 + c) * dq2
            q = qv_ref[:, col:col + dq2].astype(BF16)
            s = lax.dot_general(keys_ref[h, c], q, (((1,), (1,)), ((), ())),
                                preferred_element_type=F32)
            sc.append(s)
            tops.append(_top_values(s, PEER_TOPK))
        v1, v2 = tops
        cands = [v1[p1] + v2[p2] for p1 in range(PEER_TOPK) for p2 in range(PEER_TOPK)
                 if (p1 + 1) * (p2 + 1) <= PEER_TOPK]
        pad = (-len(cands)) % SUBLANES
        cand = jnp.concatenate(cands + [jnp.full_like(cands[0], NEG_INF)] * pad, axis=0)
        best = _top_values(cand, PEER_TOPK)
        top = v1[0] + v2[0]
        z = jnp.ones_like(top)
        for p in range(1, PEER_TOPK):
            z = z + jnp.exp(best[p] - top)
        thr_rows.append(best[PEER_TOPK - 1])
        s1_ref[h] = sc[0]
        s2_ref[h] = sc[1]
        e1_ref[h] = jnp.exp(sc[0] - v1[0]) / z
        e2_ref[h] = jnp.exp(sc[1] - v2[0])
    thr_ref[...] = jnp.concatenate(thr_rows, axis=0)


def peer_route(qv, keys, tt=256):
    n, w = qv.shape
    nh, _, nk, dq2 = keys.shape
    return pl.pallas_call(
        _peer_route_kernel,
        out_shape=(jax.ShapeDtypeStruct((nh, nk, n), F32),) * 4 + (jax.ShapeDtypeStruct((nh, n), F32),),
        grid=(n // tt,),
        in_specs=[pl.BlockSpec((tt, w), lambda i: (i, 0)),
                  pl.BlockSpec((nh, 2, nk, dq2), lambda i: (0, 0, 0, 0))],
        out_specs=(pl.BlockSpec((nh, nk, tt), lambda i: (0, 0, i)),) * 4
                  + (pl.BlockSpec((nh, tt), lambda i: (0, i)),),
        compiler_params=_params("arbitrary"),
    )(qv, keys)


def _peer_dense_kernel(rows, x_ref, u_ref, v_ref, s1_ref, e1_ref, s2_ref, e2_ref, thr_ref,
                       o_ref, wt_ref):
    j = pl.program_id(1)

    @pl.when(j == 0)
    def _():
        o_ref[...] = jnp.zeros_like(o_ref)

    st = lax.dot_general(u_ref[...], x_ref[...], (((1,), (1,)), ((), ())), preferred_element_type=F32)
    nk = s2_ref.shape[1]
    for r in range(rows):
        g = None
        row = j * rows + r
        for h in range(PEER_HEADS):
            ssum = s1_ref[h, pl.ds(row, 1), :] + s2_ref[h]
            wgt = e1_ref[h, pl.ds(row, 1), :] * e2_ref[h]
            term = jnp.where(ssum >= thr_ref[h:h + 1, :], wgt, 0.0)
            g = term if g is None else g + term
        act = _gelu_tanh(st[r * nk:(r + 1) * nk, :])
        wt_ref[r * nk:(r + 1) * nk, :] = (g * act).astype(BF16)
    o_ref[...] += lax.dot_general(wt_ref[...], v_ref[...], (((0,), (0,)), ((), ())),
                                  preferred_element_type=F32)


def peer_dense(x, u_tab, v_tab, s1, e1, s2, e2, thr, tt=512, rows=4):
    n, d = x.shape
    ne = u_tab.shape[0]
    nh, nk = s1.shape[0], s1.shape[1]
    ec = rows * nk
    once = pl.Buffered(1)
    return pl.pallas_call(
        functools.partial(_peer_dense_kernel, rows),
        out_shape=jax.ShapeDtypeStruct((n, d), F32),
        grid=(n // tt, ne // ec),
        in_specs=[pl.BlockSpec((tt, d), lambda i, j: (i, 0), pipeline_mode=once),
                  pl.BlockSpec((ec, d), lambda i, j: (j, 0)),
                  pl.BlockSpec((ec, d), lambda i, j: (j, 0)),
                  pl.BlockSpec((nh, nk, tt), lambda i, j: (0, 0, i), pipeline_mode=once),
                  pl.BlockSpec((nh, nk, tt), lambda i, j: (0, 0, i), pipeline_mode=once),
                  pl.BlockSpec((nh, nk, tt), lambda i, j: (0, 0, i), pipeline_mode=once),
                  pl.BlockSpec((nh, nk, tt), lambda i, j: (0, 0, i), pipeline_mode=once),
                  pl.BlockSpec((nh, tt), lambda i, j: (0, i))],
        out_specs=pl.BlockSpec((tt, d), lambda i, j: (i, 0)),
        scratch_shapes=[pltpu.VMEM((ec, tt), BF16)],
        compiler_params=_params("parallel", "arbitrary"),
    )(x, u_tab, v_tab, s1, e1, s2, e2, thr)


def _final_kernel(tiles_prompt, tiles_per_sample, x_ref, p_ref, g2_ref, fg_ref, yp_ref, ys_ref):
    i = pl.program_id(0)
    r = _mod_row(i, tiles_prompt, tiles_per_sample)
    x = x_ref[...] + g2_ref[pl.ds(r, 1), :] * p_ref[...]
    y = x * lax.rsqrt(jnp.mean(x * x, axis=-1, keepdims=True) + EPS) * fg_ref[...]

    @pl.when(i < tiles_prompt)
    def _():
        yp_ref[...] = y

    @pl.when(i >= tiles_prompt)
    def _():
        ys_ref[...] = y


def final_norm(x1, peer_out, g2, final_g, n_p, rows_per_sample, tm=256):
    n, d = x1.shape
    tp = n_p // tm
    kern = functools.partial(_final_kernel, tp, rows_per_sample // tm)
    return pl.pallas_call(
        kern,
        out_shape=(jax.ShapeDtypeStruct((n_p, d), F32), jax.ShapeDtypeStruct((n - n_p, d), F32)),
        grid=(n // tm,),
        in_specs=[pl.BlockSpec((tm, d), lambda i: (i, 0)),
                  pl.BlockSpec((tm, d), lambda i: (i, 0)),
                  pl.BlockSpec((SUBLANES, d), lambda i: (0, 0)),
                  pl.BlockSpec((1, d), lambda i: (0, 0))],
        out_specs=(pl.BlockSpec((tm, d), lambda i: (jnp.minimum(i, tp - 1), 0)),
                   pl.BlockSpec((tm, d), lambda i: (jnp.maximum(i - tp, 0), 0))),
        compiler_params=_params("arbitrary"),
    )(x1, peer_out, g2, final_g)


def _block_diag_in(bb_re, bb_im):
    nd, g, gc, p = bb_re.shape
    nb = S5_BLOCK_GROUPS
    eye = jnp.eye(nb, dtype=F32)

    def one(b):
        b = b.reshape(nd, g // nb, nb, gc, p)
        return jnp.einsum('djgmp,gh->djgmhp', b, eye).reshape(nd, g // nb, nb * gc, nb * p)

    return jnp.concatenate([one(bb_re), one(bb_im)], axis=-1)


def _block_diag_out(c_re, c_im):
    nd, g, gc, p = c_re.shape
    nb = S5_BLOCK_GROUPS
    eye = jnp.eye(nb, dtype=F32)

    def one(c):
        c = c.reshape(nd, g // nb, nb, gc, p)
        return jnp.einsum('djgmp,gh->djgphm', c, eye).reshape(nd, g // nb, nb * p, nb * gc)

    return jnp.concatenate([one(c_re), one(-c_im)], axis=-2)


def _state_to_blocks(s):
    b, nd, g, p = s.shape
    nb = S5_BLOCK_GROUPS
    return s.reshape(b, nd, g // nb, nb * p).transpose(1, 2, 0, 3)


def _blocks_to_state(f):
    nd, nj, b, hp = f.shape
    return f.transpose(2, 0, 1, 3).reshape(b, 1, nd, nj * S5_BLOCK_GROUPS, hp // S5_BLOCK_GROUPS)


def kernel(x_prompt, x_sample, state_s5_re, state_s5_im, state_mlstm_c, state_mlstm_n, state_mlstm_m,
           c, c_ctx, w_mod, b_mod, norm1_g, norm2_g, w_in, gate_bias, s5_lambda_re, s5_lambda_im,
           s5_log_dt, s5_b_re, s5_b_im, s5_c_re, s5_c_im, s5_d, s5_w_glu, mlstm_norm_g, w_out,
           peer_w_q, peer_keys, peer_u, peer_v, final_g):
    bp, lp, dm = x_prompt.shape
    bs, ls, _ = x_sample.shape
    n_p, n_s = bp * lp, bs * ls
    rows = ls // GRID_W
    g_s5 = s5_lambda_re.shape[2]
    p_s5 = s5_lambda_re.shape[3]
    s5_w = g_s5 * S5_GROUP_CH
    nh = MLSTM_HEADS
    dk = state_mlstm_c.shape[4]
    dv = state_mlstm_c.shape[5]
    qk_w, v_w = nh * dk, nh * dv
    off_q = s5_w
    off_k = off_q + qk_w
    off_v = off_k + qk_w
    off_o = off_v + v_w
    off_g = off_o + v_w
    n_gates = 4 * nh
    nseg = ls // S5_SEQ
    cps = GRID_W // nseg
    assert bs + 1 <= SUBLANES and lp == MLSTM_T and ls % MLSTM_T == 0
    assert lp == S5_SEQ and bp % SUBLANES == 0 and bs * nseg == SUBLANES and cps * rows == S5_SEQ

    xp = x_prompt.reshape(n_p, dm)
    xs = x_sample.reshape(n_s, dm)

    cvec = jnp.zeros((SUBLANES, dm), F32).at[0].set(c_ctx).at[1:1 + bs].set(c)
    mod = modulation(cvec, w_mod[0], b_mod[0])
    sh1, sc1, g1, sh2, sc2, g2 = [mod[:, i * dm:(i + 1) * dm] for i in range(6)]

    h1 = norm1(xp, xs, norm1_g, sc1, sh1, ls)
    proj = matmul(h1, w_in[0], off_g, tm=1024, tn=512)
    w_gate = jnp.pad(w_in[0][:, off_g:].astype(BF16), ((0, 0), (0, LANES - n_gates)))
    gate_b = jnp.pad(gate_bias[0].reshape(1, n_gates), ((0, 0), (0, LANES - n_gates)))
    gates_col = matmul_bias(h1, w_gate, gate_b, tm=1024)
    gates_row = gates_col[:, :n_gates].T

    lam_re = s5_lambda_re[0].reshape(2 * g_s5, p_s5)
    lam_im = s5_lambda_im[0].reshape(2 * g_s5, p_s5)
    log_dt = s5_log_dt[0].reshape(2 * g_s5, 1)
    b_re_t = s5_b_re[0].transpose(0, 1, 3, 2).reshape(2 * g_s5, S5_GROUP_CH, p_s5)
    b_im_t = s5_b_im[0].transpose(0, 1, 3, 2).reshape(2 * g_s5, S5_GROUP_CH, p_s5)
    ab_re, ab_im, bb_re, bb_im = s5_discretise(lam_re, lam_im, log_dt, b_re_t, b_im_t)
    nj = g_s5 // S5_BLOCK_GROUPS
    half = S5_BLOCK_GROUPS * p_s5
    bb = _block_diag_in(bb_re.reshape(2, g_s5, S5_GROUP_CH, p_s5),
                        bb_im.reshape(2, g_s5, S5_GROUP_CH, p_s5)).astype(BF16)
    cm = _block_diag_out(s5_c_re[0], s5_c_im[0]).astype(BF16)
    a_re = ab_re.reshape(2, nj, 1, half)
    a_im = ab_im.reshape(2, nj, 1, half)
    w_glu = s5_w_glu[0].astype(BF16)

    zero_p = jnp.zeros((2, nj, bp, half), F32)
    y_p, fin_re, fin_im = s5_scan(proj, bp, bb, cm, a_re, a_im, zero_p, zero_p, True)
    y_s5_p = s5_glu(y_p.reshape(2, lp, bp * s5_w), 2, lambda i, s: (0, 2 * i + s),
                    proj, 0, n_p, s5_d, w_glu)

    u_seg = (proj[n_p:, :s5_w].reshape(bs, rows, nseg, cps, s5_w).transpose(0, 2, 3, 1, 4)
             .reshape(bs * nseg * S5_SEQ, s5_w))
    zero_s = jnp.zeros((2, nj, SUBLANES, half), F32)
    f_re, f_im = s5_scan(u_seg, SUBLANES, bb, cm, a_re, a_im, zero_s, zero_s, False)
    i_re, i_im = s5_chain(f_re, f_im, a_re, a_im, _state_to_blocks(state_s5_re[:, 0]),
                          _state_to_blocks(state_s5_im[:, 0]), bs, nseg)
    y_seg, _, _ = s5_scan(u_seg, SUBLANES, bb, cm, a_re, a_im, i_re, i_im, True)
    y_s = (y_seg.reshape(2, cps, rows, bs, nseg, s5_w).transpose(0, 3, 2, 4, 1, 5)
           .reshape(2, n_s, s5_w))
    y_s5_s = s5_glu(y_s, 1, lambda i, s: (i, 0), proj, n_p // 512, n_s, s5_d, w_glu)

    hf_p, hb_p, c_end, n_end, m_end = mlstm(proj, gates_col, gates_row, 0, bp, lp,
                                            off_q, off_k, off_v, dk, dv)
    m0 = jnp.broadcast_to(state_mlstm_m[:, 0].reshape(bs, 2 * nh, 1), (bs, 2 * nh, LANES))
    hf_s, hb_s = mlstm(proj, gates_col, gates_row, n_p, bs, ls, off_q, off_k, off_v, dk, dv,
                       state=(state_mlstm_c[:, 0], state_mlstm_n[:, 0].reshape(bs, 2 * nh, dk), m0))
    norm_g = mlstm_norm_g[0].reshape(1, v_w)
    y_ml_p = mlstm_out(hf_p, hb_p, proj, 0, off_o, norm_g, dv)
    y_ml_s = mlstm_out(hf_s, hb_s, proj, n_p, off_o, norm_g, dv)

    w_out_b = w_out[0].astype(BF16)
    x1 = out_proj_residual(y_s5_p, y_s5_s, y_ml_p, y_ml_s, w_out_b[:s5_w], w_out_b[s5_w:], xp, xs, g1, ls)

    h2 = norm2(x1, norm2_g, sc2, sh2, n_p, ls)
    qv = matmul(h2, peer_w_q[0], peer_w_q.shape[2], tm=1024, tn=512)
    s1, e1, s2, e2, thr = peer_route(qv, peer_keys[0].astype(BF16))
    peer_out = peer_dense(h2, peer_u[0].astype(BF16), peer_v[0].astype(BF16), s1, e1, s2, e2, thr)

    y_p_out, y_s_out = final_norm(x1, peer_out, g2, final_g.reshape(1, dm), n_p, ls)

    dt_ = x_prompt.dtype
    return (y_p_out.reshape(bp, lp, dm), y_s_out.reshape(bs, ls, dm),
            _blocks_to_state(fin_re).astype(dt_), _blocks_to_state(fin_im).astype(dt_),
            c_end.reshape(bp, 1, 2, nh, dk, dv).astype(dt_),
            n_end.reshape(bp, 1, 2, nh, dk).astype(dt_),
            m_end[:, :, 0].reshape(bp, 1, 2, nh).astype(dt_))
```
